```python
import jax, jax.numpy as jnp
from jax import lax
import numpy as np

D_MODEL = 2048
BATCH = 1
SEQ = 8192
DEPTH = 2
DEC_BATCH = 2
DEC_SEQ = 16384
PAST_LEN = 128

N_MIXERS = 2
EXPAND = 2
E_WIDTH = EXPAND * D_MODEL
GRID_W = 64
CHUNK = 128
A_GROUPS = 16
A_GROUP_DIM = E_WIDTH // A_GROUPS
B_HEADS = 32
B_HEAD_DIM = E_WIDTH // B_HEADS
NA_ROWS = 8
NA_COLS = 16
N_A_LAYERS = (DEPTH + 1) // 2
N_B_LAYERS = DEPTH // 2
EPS = 1e-6

kernel_name = "hybrid_gmlp_natten_encoder"


def rmsnorm(x, g):
    xf = x.astype(jnp.float32)
    inv = lax.rsqrt(jnp.mean(xf * xf, axis=-1, keepdims=True) + EPS)
    return (xf * inv).astype(x.dtype) * g


def gmlp_mixer(h, w_in, v_g, w_s, b_s, w_out):
    bsz, t, _ = h.shape
    proj = h @ w_in
    u, v, z = jnp.split(proj, 3, axis=-1)
    u = jax.nn.gelu(u)
    v = rmsnorm(jax.nn.gelu(v), v_g)
    v = v.reshape(bsz, t // CHUNK, CHUNK, A_GROUPS, A_GROUP_DIM)
    s = jnp.einsum('gpq,bnqgc->bnpgc', w_s, v) + b_s.T[None, None, :, :, None]
    s = s.reshape(bsz, t, E_WIDTH)
    y = u * s * jax.nn.silu(z)
    return y @ w_out


def na_mixer(h, w_in, rpb, w_out):
    bsz, t, _ = h.shape
    rows = t // GRID_W
    kh = min(NA_ROWS, rows)
    kw = NA_COLS
    proj = h @ w_in
    q, k, v, z = jnp.split(proj, 4, axis=-1)
    shp = (bsz, rows, GRID_W, B_HEADS, B_HEAD_DIM)
    q = q.reshape(shp) * (B_HEAD_DIM ** -0.5)
    k = k.reshape(shp)
    v = v.reshape(shp)
    col = jnp.arange(GRID_W)
    col_start = jnp.clip(col - kw // 2, 0, GRID_W - kw)
    col_idx = col_start[:, None] + jnp.arange(kw)[None, :]
    col_off = col_idx - col[:, None] + (NA_COLS - 1)

    def row_block(r):
        r0 = jnp.clip(r - kh // 2, 0, rows - kh)
        k_rows = lax.dynamic_slice_in_dim(k, r0, kh, axis=1)
        v_rows = lax.dynamic_slice_in_dim(v, r0, kh, axis=1)
        k_win = k_rows[:, :, col_idx]
        v_win = v_rows[:, :, col_idx]
        q_r = lax.dynamic_index_in_dim(q, r, axis=1, keepdims=False)
        sc = jnp.einsum('bchd,bicjhd->bhcij', q_r, k_win).astype(jnp.float32)
        row_off = r0 + jnp.arange(kh) - r + (NA_ROWS - 1)
        bias = rpb[:, row_off[None, :, None], col_off[:, None, :]]
        sc = sc + bias[None].astype(jnp.float32)
        p = jax.nn.softmax(sc.reshape(bsz, B_HEADS, GRID_W, kh * kw), axis=-1)
        p = p.reshape(bsz, B_HEADS, GRID_W, kh, kw).astype(v.dtype)
        return jnp.einsum('bhcij,bicjhd->bchd', p, v_win)

    out = lax.map(row_block, jnp.arange(rows))
    out = jnp.moveaxis(out, 0, 1).reshape(bsz, t, E_WIDTH)
    y = out * jax.nn.silu(z)
    return y @ w_out


def trunk(x, norm_g, final_g, a_w_in, a_v_g, a_ws, a_bs, a_w_out, b_w_in, b_rpb, b_w_out):
    for i in range(DEPTH):
        h = rmsnorm(x, norm_g[i])
        j = i // N_MIXERS
        if i % N_MIXERS == 0:
            y = gmlp_mixer(h, a_w_in[j], a_v_g[j], a_ws[j], a_bs[j], a_w_out[j])
        else:
            y = na_mixer(h, b_w_in[j], b_rpb[j], b_w_out[j])
        x = x + y
    return rmsnorm(x, final_g)


def setup_inputs(seed: int = 0) -> dict:
    key = jax.random.key(seed)
    ks = jax.random.split(key, 12)
    f32 = jnp.float32
    x_prompt = jax.random.normal(ks[0], (BATCH, SEQ, D_MODEL), f32)
    x_sample = jax.random.normal(ks[1], (DEC_BATCH, DEC_SEQ, D_MODEL), f32)
    norm_g = 1.0 + 0.02 * jax.random.normal(ks[2], (DEPTH, D_MODEL), f32)
    final_g = 1.0 + 0.02 * jax.random.normal(ks[3], (D_MODEL,), f32)
    a_w_in = jax.random.normal(ks[4], (N_A_LAYERS, D_MODEL, 3 * E_WIDTH), f32) * D_MODEL ** -0.5
    a_v_g = 1.0 + 0.02 * jax.random.normal(ks[5], (N_A_LAYERS, E_WIDTH), f32)
    a_ws = jax.random.normal(ks[6], (N_A_LAYERS, A_GROUPS, CHUNK, CHUNK), f32) * CHUNK ** -0.5
    a_bs = 1.0 + 0.02 * jax.random.normal(ks[7], (N_A_LAYERS, A_GROUPS, CHUNK), f32)
    a_w_out = jax.random.normal(ks[8], (N_A_LAYERS, E_WIDTH, D_MODEL), f32) * E_WIDTH ** -0.5
    b_w_in = jax.random.normal(ks[9], (N_B_LAYERS, D_MODEL, 4 * E_WIDTH), f32) * D_MODEL ** -0.5
    b_rpb = 0.02 * jax.random.normal(ks[10], (N_B_LAYERS, B_HEADS, 2 * NA_ROWS - 1, 2 * NA_COLS - 1), f32)
    b_w_out = jax.random.normal(ks[11], (N_B_LAYERS, E_WIDTH, D_MODEL), f32) * E_WIDTH ** -0.5
    return {"x_prompt": x_prompt, "x_sample": x_sample, "norm_g": norm_g, "final_g": final_g,
            "a_w_in": a_w_in, "a_v_g": a_v_g, "a_ws": a_ws, "a_bs": a_bs, "a_w_out": a_w_out,
            "b_w_in": b_w_in, "b_rpb": b_rpb, "b_w_out": b_w_out}


def reference(x_prompt, x_sample, norm_g, final_g, a_w_in, a_v_g, a_ws, a_bs, a_w_out, b_w_in, b_rpb, b_w_out):
    y_prompt = trunk(x_prompt, norm_g, final_g, a_w_in, a_v_g, a_ws, a_bs, a_w_out, b_w_in, b_rpb, b_w_out)
    y_sample = trunk(x_sample, norm_g, final_g, a_w_in, a_v_g, a_ws, a_bs, a_w_out, b_w_in, b_rpb, b_w_out)
    return (y_prompt, y_sample)
```

```python
import functools

import jax
import jax.numpy as jnp
from jax import lax
from jax.experimental import pallas as pl
from jax.experimental.pallas import tpu as pltpu

F32 = jnp.float32
BF16 = jnp.bfloat16

EXPAND = 2
GRID_W = 64
CHUNK = 128
A_GROUPS = 16
B_HEADS = 32
NA_ROWS = 8
NA_COLS = 16
EPS = 1e-6

LANES = 128
VMEM_LIMIT_BYTES = 56 * 1024 * 1024

Q_ROWS = 8
SUB_ROWS = 2
WIN_ROWS = 10
NEG = -1e30


def _gelu_tanh(x):
    c = 0.7978845608028654
    return x * (0.5 * (1.0 + jnp.tanh(c * (x + 0.044715 * (x * x * x)))))


def _silu(x):
    return x * (1.0 / (1.0 + jnp.exp(-x)))


def _norm_matmul_kernel(x_ref, g_ref, w_ref, o_ref, h_ref, *, n_scaled_tiles, scale, col_major):
    j = pl.program_id(1)

    @pl.when(j == 0)
    def _():
        xf = x_ref[...]
        inv = lax.rsqrt(jnp.mean(xf * xf, axis=-1, keepdims=True) + EPS)
        h_ref[...] = ((xf * inv) * g_ref[...]).astype(BF16)

    acc = jnp.dot(h_ref[...], w_ref[...], preferred_element_type=F32)
    if n_scaled_tiles:
        acc = acc * jnp.where(j < n_scaled_tiles, scale, 1.0).astype(F32)
    if col_major:
        for c in range(o_ref.shape[0]):
            o_ref[c] = acc[:, c * LANES:(c + 1) * LANES].astype(o_ref.dtype)
    else:
        o_ref[...] = acc.astype(o_ref.dtype)


def _norm_matmul(x, g, w, *, tm, tn, n_scaled_tiles=0, scale=1.0, col_major=False):
    m, d = x.shape
    n = w.shape[1]
    assert m % tm == 0 and n % tn == 0 and tn % LANES == 0
    if col_major:
        out_shape = jax.ShapeDtypeStruct((n // LANES, m, LANES), BF16)
        out_spec = pl.BlockSpec((tn // LANES, tm, LANES), lambda i, j: (j, i, 0))
    else:
        out_shape = jax.ShapeDtypeStruct((m, n), BF16)
        out_spec = pl.BlockSpec((tm, tn), lambda i, j: (i, j))
    return pl.pallas_call(
        functools.partial(_norm_matmul_kernel, n_scaled_tiles=n_scaled_tiles, scale=scale,
                          col_major=col_major),
        grid=(m // tm, n // tn),
        in_specs=[
            pl.BlockSpec((tm, d), lambda i, j: (i, 0)),
            pl.BlockSpec((1, d), lambda i, j: (0, 0)),
            pl.BlockSpec((d, tn), lambda i, j: (0, j)),
        ],
        out_specs=out_spec,
        out_shape=out_shape,
        scratch_shapes=[pltpu.VMEM((tm, d), BF16)],
        compiler_params=pltpu.CompilerParams(
            dimension_semantics=("arbitrary", "arbitrary"),
            vmem_limit_bytes=VMEM_LIMIT_BYTES),
        name="norm_in_proj",
    )(x, g.reshape(1, d), w)


def _gmlp_out_kernel(u_ref, v_ref, z_ref, vg_ref, ws_ref, bs_ref, wout_ref, x_ref, o_ref,
                     gv_ref, y_ref):
    tm, e = u_ref.shape
    gdim = e // A_GROUPS
    gv = _gelu_tanh(v_ref[...].astype(F32))
    gv_ref[...] = gv
    inv = lax.rsqrt(jnp.mean(gv * gv, axis=-1, keepdims=True) + EPS)
    for g in range(A_GROUPS):
        cols = slice(g * gdim, (g + 1) * gdim)
        vn = ((gv_ref[:, cols] * inv) * vg_ref[:, cols]).astype(BF16)
        for c in range(tm // CHUNK):
            rows = slice(c * CHUNK, (c + 1) * CHUNK)
            s = jnp.dot(ws_ref[g], vn[rows], preferred_element_type=F32) + bs_ref[g]
            u = _gelu_tanh(u_ref[rows, cols].astype(F32))
            z = z_ref[rows, cols].astype(F32)
            y_ref[rows, cols] = ((u * s) * _silu(z)).astype(BF16)
    out = jnp.dot(y_ref[...], wout_ref[...], preferred_element_type=F32)
    o_ref[...] = x_ref[...] + out


def _gmlp_out(proj, v_g, w_s, b_s, w_out, x, *, tm):
    m, d = x.shape
    e = w_out.shape[0]
    assert m % tm == 0 and tm % CHUNK == 0
    return pl.pallas_call(
        _gmlp_out_kernel,
        grid=(m // tm,),
        in_specs=[
            pl.BlockSpec((tm, e), lambda i: (i, 0)),
            pl.BlockSpec((tm, e), lambda i: (i, 1)),
            pl.BlockSpec((tm, e), lambda i: (i, 2)),
            pl.BlockSpec((1, e), lambda i: (0, 0)),
            pl.BlockSpec((A_GROUPS, CHUNK, CHUNK), lambda i: (0, 0, 0)),
            pl.BlockSpec((A_GROUPS, CHUNK, 1), lambda i: (0, 0, 0)),
            pl.BlockSpec((e, d), lambda i: (0, 0), pipeline_mode=pl.Buffered(1)),
            pl.BlockSpec((tm, d), lambda i: (i, 0)),
        ],
        out_specs=pl.BlockSpec((tm, d), lambda i: (i, 0)),
        out_shape=jax.ShapeDtypeStruct((m, d), F32),
        scratch_shapes=[pltpu.VMEM((tm, e), F32), pltpu.VMEM((tm, e), BF16)],
        compiler_params=pltpu.CompilerParams(
            dimension_semantics=("arbitrary",),
            vmem_limit_bytes=VMEM_LIMIT_BYTES),
        name="gmlp_gate_out_proj",
    )(proj, proj, proj, v_g.reshape(1, e), w_s, b_s.reshape(A_GROUPS, CHUNK, 1), w_out, x)


_BIAS_VARIANTS = ((0, (0, 0)), (-2, (0, 0)), (-4, (0, 1)), (-6, (2, 2)), (-8, (2, 2)))
_PAIR_MIN = -9
_PAIR_ROWS = 24


def _build_bias(rp_ref, bias_ref):
    c = lax.broadcasted_iota(jnp.int32, (GRID_W, LANES), 0)
    lane = lax.broadcasted_iota(jnp.int32, (GRID_W, LANES), 1)
    kc = lane & (GRID_W - 1)
    cs = jnp.clip(c - NA_COLS // 2, 0, GRID_W - NA_COLS)
    col_ok = (kc >= cs) & (kc < cs + NA_COLS)
    left = lane < GRID_W
    pair_cache = {}

    def pair(d):
        if d not in pair_cache:
            row = rp_ref[0, d - _PAIR_MIN:d - _PAIR_MIN + 1, :]
            x = jnp.broadcast_to(row, (GRID_W, LANES))
            pair_cache[d] = pltpu.roll(x, LANES - (NA_COLS - 1), 1, stride=1, stride_axis=0)
        return pair_cache[d]

    for v, (delta, los) in enumerate(_BIAS_VARIANTS):
        for a in range(SUB_ROWS):
            lo = los[a]
            for p in range(WIN_ROWS // 2):
                ok_l = lo <= 2 * p <= lo + NA_ROWS - 1
                ok_r = lo <= 2 * p + 1 <= lo + NA_ROWS - 1
                if ok_l and ok_r:
                    mask = col_ok
                elif ok_l:
                    mask = col_ok & left
                elif ok_r:
                    mask = col_ok & jnp.logical_not(left)
                else:
                    mask = None
                if mask is None:
                    blk = jnp.full((GRID_W, LANES), NEG, F32)
                else:
                    blk = jnp.where(mask, pair(delta + 2 * p - a), NEG)
                bias_ref[v, a * GRID_W:(a + 1) * GRID_W, p * LANES:(p + 1) * LANES] = blk


def _na_kernel(q_ref, k_ref, v_ref, z_ref, rp_ref, y_ref, bias_ref, *, rows):
    b = pl.program_id(1)
    rb = pl.program_id(2)
    n_rb = pl.num_programs(2)

    @pl.when((b == 0) & (rb == 0))
    def _():
        _build_bias(rp_ref, bias_ref)

    sub_tokens = SUB_ROWS * GRID_W
    win_tokens = WIN_ROWS * GRID_W
    n_sub = Q_ROWS // SUB_ROWS
    for i in range(n_sub):
        r_a = rb * Q_ROWS + i * SUB_ROWS
        ws = jnp.minimum(jnp.maximum(r_a - NA_ROWS // 2, 0), rows - WIN_ROWS)
        if i == 0:
            variant = jnp.where(rb == 0, 0, 2)
        elif i == 1:
            variant = jnp.where(rb == 0, 1, 2)
        elif i == n_sub - 2:
            variant = jnp.where(rb == n_rb - 1, 3, 2)
        elif i == n_sub - 1:
            variant = jnp.where(rb == n_rb - 1, 4, 2)
        else:
            variant = 2
        start = pl.multiple_of(ws * GRID_W, GRID_W)
        q = q_ref[0, i * sub_tokens:(i + 1) * sub_tokens, :]
        kw = k_ref[0, 0, pl.ds(start, win_tokens), :]
        vw = v_ref[0, 0, pl.ds(start, win_tokens), :]
        s = lax.dot_general(q, kw, (((1,), (1,)), ((), ())), preferred_element_type=F32)
        s = s + bias_ref[variant]
        m = jnp.max(s, axis=-1, keepdims=True)
        p = jnp.exp(s - m)
        l = jnp.sum(p, axis=-1, keepdims=True)
        o = jnp.dot(p.astype(BF16), vw, preferred_element_type=F32)
        o = o / l
        z = z_ref[0, i * sub_tokens:(i + 1) * sub_tokens, :].astype(F32)
        y_ref[i * sub_tokens:(i + 1) * sub_tokens, :] = (o * _silu(z)).astype(BF16)


def _na_attention(proj, rp, *, bsz, t):
    n_blocks, m, hd = proj.shape
    assert n_blocks == 4 * B_HEADS and m == bsz * t and hd == LANES
    rows = t // GRID_W
    assert t % (Q_ROWS * GRID_W) == 0 and rows >= 2 * Q_ROWS and rows >= WIN_ROWS
    assert Q_ROWS // SUB_ROWS >= 4 and NA_ROWS + SUB_ROWS - 1 <= WIN_ROWS
    n_rb = rows // Q_ROWS
    blk = Q_ROWS * GRID_W
    proj4 = proj.reshape(n_blocks, bsz, t, hd)
    return pl.pallas_call(
        functools.partial(_na_kernel, rows=rows),
        grid=(B_HEADS, bsz, n_rb),
        in_specs=[
            pl.BlockSpec((1, blk, hd), lambda h, b, r: (h, b * n_rb + r, 0)),
            pl.BlockSpec((1, 1, t, hd), lambda h, b, r: (B_HEADS + h, b, 0, 0)),
            pl.BlockSpec((1, 1, t, hd), lambda h, b, r: (2 * B_HEADS + h, b, 0, 0)),
            pl.BlockSpec((1, blk, hd), lambda h, b, r: (3 * B_HEADS + h, b * n_rb + r, 0)),
            pl.BlockSpec((1, _PAIR_ROWS, LANES), lambda h, b, r: (h, 0, 0)),
        ],
        out_specs=pl.BlockSpec((blk, hd), lambda h, b, r: (b * n_rb + r, h)),
        out_shape=jax.ShapeDtypeStruct((m, B_HEADS * hd), BF16),
        scratch_shapes=[pltpu.VMEM((len(_BIAS_VARIANTS), SUB_ROWS * GRID_W, WIN_ROWS * GRID_W), F32)],
        compiler_params=pltpu.CompilerParams(
            dimension_semantics=("arbitrary", "arbitrary", "arbitrary"),
            vmem_limit_bytes=VMEM_LIMIT_BYTES),
        name="na_attention",
    )(proj, proj4, proj4, proj, rp)


def _pair_table(rpb):
    h, nr, nc = rpb.shape
    n_off = _PAIR_ROWS + 1
    first = -(NA_ROWS - 1) - _PAIR_MIN
    ext = jnp.zeros((h, n_off, GRID_W), F32)
    ext = ext.at[:, first:first + nr, :nc].set(rpb.astype(F32))
    return jnp.concatenate([ext[:, :-1], ext[:, 1:]], axis=-1)


def _out_norm_kernel(y_ref, w_ref, x_ref, g_ref, o_ref):
    acc = jnp.dot(y_ref[...], w_ref[...], preferred_element_type=F32)
    xn = x_ref[...] + acc
    inv = lax.rsqrt(jnp.mean(xn * xn, axis=-1, keepdims=True) + EPS)
    o_ref[...] = (xn * inv) * g_ref[...]


def _out_norm(y, w_out, x, g, *, tm):
    m, d = x.shape
    e = w_out.shape[0]
    assert m % tm == 0
    return pl.pallas_call(
        _out_norm_kernel,
        grid=(m // tm,),
        in_specs=[
            pl.BlockSpec((tm, e), lambda i: (i, 0)),
            pl.BlockSpec((e, d), lambda i: (0, 0), pipeline_mode=pl.Buffered(1)),
            pl.BlockSpec((tm, d), lambda i: (i, 0)),
            pl.BlockSpec((1, d), lambda i: (0, 0)),
        ],
        out_specs=pl.BlockSpec((tm, d), lambda i: (i, 0)),
        out_shape=jax.ShapeDtypeStruct((m, d), F32),
        compiler_params=pltpu.CompilerParams(
            dimension_semantics=("arbitrary",),
            vmem_limit_bytes=VMEM_LIMIT_BYTES),
        name="out_proj_final_norm",
    )(y, w_out, x, g.reshape(1, d))


def _trunk(x, norm_g, final_g, a_w_in, a_v_g, a_ws, a_bs, a_w_out, b_w_in, rp, b_w_out):
    bsz, t, d = x.shape
    x2 = x.reshape(bsz * t, d)
    e = a_w_out.shape[0]
    head_dim = e // B_HEADS
    proj_a = _norm_matmul(x2, norm_g[0], a_w_in, tm=1024, tn=1024)
    x2 = _gmlp_out(proj_a, a_v_g, a_ws, a_bs, a_w_out, x2, tm=256)
    proj_b = _norm_matmul(x2, norm_g[1], b_w_in, tm=1024, tn=1024,
                          n_scaled_tiles=e // 1024, scale=head_dim ** -0.5, col_major=True)
    y = _na_attention(proj_b, rp, bsz=bsz, t=t)
    out = _out_norm(y, b_w_out, x2, final_g, tm=512)
    return out.reshape(bsz, t, d)


def kernel(x_prompt, x_sample, norm_g, final_g, a_w_in, a_v_g, a_ws, a_bs, a_w_out, b_w_in, b_rpb,
           b_w_out):
    assert norm_g.shape[0] == 2 and a_w_in.shape[0] == 1 and b_w_in.shape[0] == 1
    params = (norm_g, final_g, a_w_in[0].astype(BF16), a_v_g[0], a_ws[0].astype(BF16), a_bs[0],
              a_w_out[0].astype(BF16), b_w_in[0].astype(BF16), _pair_table(b_rpb[0]),
              b_w_out[0].astype(BF16))
    return (_trunk(x_prompt, *params), _trunk(x_sample, *params))
```

```python
import functools

import jax
import jax.numpy as jnp
from jax import lax
from jax.experimental import pallas as pl
from jax.experimental.pallas import tpu as pltpu

F32 = jnp.float32
BF16 = jnp.bfloat16

EXPAND = 2
GRID_W = 64
CHUNK = 128
A_GROUPS = 16
B_HEADS = 32
NA_ROWS = 8
NA_COLS = 16
EPS = 1e-6

LANES = 128
VMEM_LIMIT_BYTES = 56 * 1024 * 1024

Q_ROWS = 16
SUB_ROWS = 2
WIN_ROWS = 10
NEG = -1e30
LOG2E = 1.4426950408889634


def _gelu_tanh(x):
    c = 0.7978845608028654
    return x * (0.5 * (1.0 + jnp.tanh(c * (x + 0.044715 * (x * x * x)))))


def _silu(x):
    return x * (1.0 / (1.0 + jnp.exp(-x)))


def _norm_matmul_kernel(x_ref, g_ref, w_ref, o_ref, h_ref, *, n_scaled_tiles, scale, col_major):
    j = pl.program_id(1)

    @pl.when(j == 0)
    def _():
        xf = x_ref[...]
        inv = lax.rsqrt(jnp.mean(xf * xf, axis=-1, keepdims=True) + EPS)
        h_ref[...] = ((xf * inv) * g_ref[...]).astype(BF16)

    acc = jnp.dot(h_ref[...], w_ref[...], preferred_element_type=F32)
    if n_scaled_tiles:
        acc = acc * jnp.where(j < n_scaled_tiles, scale, 1.0).astype(F32)
    if col_major:
        for c in range(o_ref.shape[0]):
            o_ref[c] = acc[:, c * LANES:(c + 1) * LANES].astype(o_ref.dtype)
    else:
        o_ref[...] = acc.astype(o_ref.dtype)


def _norm_matmul(x, g, w, *, tm, tn, n_scaled_tiles=0, scale=1.0, col_major=False):
    m, d = x.shape
    n = w.shape[1]
    assert m % tm == 0 and n % tn == 0 and tn % LANES == 0
    if col_major:
        out_shape = jax.ShapeDtypeStruct((n // LANES, m, LANES), BF16)
        out_spec = pl.BlockSpec((tn // LANES, tm, LANES), lambda i, j: (j, i, 0))
    else:
        out_shape = jax.ShapeDtypeStruct((m, n), BF16)
        out_spec = pl.BlockSpec((tm, tn), lambda i, j: (i, j))
    return pl.pallas_call(
        functools.partial(_norm_matmul_kernel, n_scaled_tiles=n_scaled_tiles, scale=scale,
                          col_major=col_major),
        grid=(m // tm, n // tn),
        in_specs=[
            pl.BlockSpec((tm, d), lambda i, j: (i, 0)),
            pl.BlockSpec((1, d), lambda i, j: (0, 0)),
            pl.BlockSpec((d, tn), lambda i, j: (0, j)),
        ],
        out_specs=out_spec,
        out_shape=out_shape,
        scratch_shapes=[pltpu.VMEM((tm, d), BF16)],
        compiler_params=pltpu.CompilerParams(
            dimension_semantics=("arbitrary", "arbitrary"),
            vmem_limit_bytes=VMEM_LIMIT_BYTES),
        name="norm_in_proj",
    )(x, g.reshape(1, d), w)


def _gmlp_out_kernel(u_ref, v_ref, z_ref, vg_ref, ws_ref, bs_ref, wout_ref, x_ref, o_ref,
                     gv_ref, y_ref):
    tm, e = u_ref.shape
    gdim = e // A_GROUPS
    gv = _gelu_tanh(v_ref[...].astype(F32))
    gv_ref[...] = gv
    inv = lax.rsqrt(jnp.mean(gv * gv, axis=-1, keepdims=True) + EPS)
    for g in range(A_GROUPS):
        cols = slice(g * gdim, (g + 1) * gdim)
        vn = ((gv_ref[:, cols] * inv) * vg_ref[:, cols]).astype(BF16)
        for c in range(tm // CHUNK):
            rows = slice(c * CHUNK, (c + 1) * CHUNK)
            s = jnp.dot(ws_ref[g], vn[rows], preferred_element_type=F32) + bs_ref[g]
            u = _gelu_tanh(u_ref[rows, cols].astype(F32))
            z = z_ref[rows, cols].astype(F32)
            y_ref[rows, cols] = ((u * s) * _silu(z)).astype(BF16)
    out = jnp.dot(y_ref[...], wout_ref[...], preferred_element_type=F32)
    o_ref[...] = x_ref[...] + out


def _gmlp_out(proj, v_g, w_s, b_s, w_out, x, *, tm):
    m, d = x.shape
    e = w_out.shape[0]
    assert m % tm == 0 and tm % CHUNK == 0
    return pl.pallas_call(
        _gmlp_out_kernel,
        grid=(m // tm,),
        in_specs=[
            pl.BlockSpec((tm, e), lambda i: (i, 0)),
            pl.BlockSpec((tm, e), lambda i: (i, 1)),
            pl.BlockSpec((tm, e), lambda i: (i, 2)),
            pl.BlockSpec((1, e), lambda i: (0, 0)),
            pl.BlockSpec((A_GROUPS, CHUNK, CHUNK), lambda i: (0, 0, 0)),
            pl.BlockSpec((A_GROUPS, CHUNK, 1), lambda i: (0, 0, 0)),
            pl.BlockSpec((e, d), lambda i: (0, 0), pipeline_mode=pl.Buffered(1)),
            pl.BlockSpec((tm, d), lambda i: (i, 0)),
        ],
        out_specs=pl.BlockSpec((tm, d), lambda i: (i, 0)),
        out_shape=jax.ShapeDtypeStruct((m, d), F32),
        scratch_shapes=[pltpu.VMEM((tm, e), F32), pltpu.VMEM((tm, e), BF16)],
        compiler_params=pltpu.CompilerParams(
            dimension_semantics=("arbitrary",),
            vmem_limit_bytes=VMEM_LIMIT_BYTES),
        name="gmlp_gate_out_proj",
    )(proj, proj, proj, v_g.reshape(1, e), w_s, b_s.reshape(A_GROUPS, CHUNK, 1), w_out, x)


_BIAS_VARIANTS = ((0, (0, 0)), (-2, (0, 0)), (-4, (0, 1)), (-6, (2, 2)), (-8, (2, 2)))
_PAIR_MIN = -9
_PAIR_ROWS = 24
_NT_DIMS = (((1,), (1,)), ((), ()))


def _build_bias(rp_ref, bias_ref):
    c = lax.broadcasted_iota(jnp.int32, (GRID_W, LANES), 0)
    lane = lax.broadcasted_iota(jnp.int32, (GRID_W, LANES), 1)
    kc = lane & (GRID_W - 1)
    cs = jnp.clip(c - NA_COLS // 2, 0, GRID_W - NA_COLS)
    col_ok = (kc >= cs) & (kc < cs + NA_COLS)
    left = lane < GRID_W
    pair_cache = {}

    def pair(d):
        if d not in pair_cache:
            row = rp_ref[0, d - _PAIR_MIN:d - _PAIR_MIN + 1, :]
            x = jnp.broadcast_to(row, (GRID_W, LANES))
            x = pltpu.roll(x, LANES - (NA_COLS - 1), 1, stride=1, stride_axis=0)
            pair_cache[d] = x * LOG2E
        return pair_cache[d]

    for v, (delta, los) in enumerate(_BIAS_VARIANTS):
        for a in range(SUB_ROWS):
            lo = los[a]
            for p in range(WIN_ROWS // 2):
                ok_l = lo <= 2 * p <= lo + NA_ROWS - 1
                ok_r = lo <= 2 * p + 1 <= lo + NA_ROWS - 1
                if ok_l and ok_r:
                    mask = col_ok
                elif ok_l:
                    mask = col_ok & left
                elif ok_r:
                    mask = col_ok & jnp.logical_not(left)
                else:
                    mask = None
                if mask is None:
                    blk = jnp.full((GRID_W, LANES), NEG, F32)
                else:
                    blk = jnp.where(mask, pair(delta + 2 * p - a), NEG)
                bias_ref[v, a * GRID_W:(a + 1) * GRID_W, p * LANES:(p + 1) * LANES] = blk


def _na_kernel(q_ref, k_ref, v_ref, z_ref, rp_ref, y_ref, bias_ref, *, rows):
    b = pl.program_id(1)
    rb = pl.program_id(2)
    n_rb = pl.num_programs(2)
    sub_tokens = SUB_ROWS * GRID_W
    win_tokens = WIN_ROWS * GRID_W
    n_sub = Q_ROWS // SUB_ROWS

    @pl.when((b == 0) & (rb == 0))
    def _():
        _build_bias(rp_ref, bias_ref)

    def window_start(i):
        r_a = rb * Q_ROWS + i * SUB_ROWS
        ws = jnp.minimum(jnp.maximum(r_a - NA_ROWS // 2, 0), rows - WIN_ROWS)
        return pl.multiple_of(ws * GRID_W, GRID_W)

    def variant(i):
        if i == 0:
            return jnp.where(rb == 0, 0, 2)
        if i == 1:
            return jnp.where(rb == 0, 1, 2)
        if i == n_sub - 2:
            return jnp.where(rb == n_rb - 1, 3, 2)
        if i == n_sub - 1:
            return jnp.where(rb == n_rb - 1, 4, 2)
        return 2

    def scores(i):
        q = q_ref[0, i * sub_tokens:(i + 1) * sub_tokens, :]
        kw = k_ref[0, 0, pl.ds(window_start(i), win_tokens), :]
        s = lax.dot_general(q, kw, _NT_DIMS, preferred_element_type=F32)
        return s + bias_ref[variant(i)]

    def finish(i, s):
        m = jnp.max(s, axis=-1, keepdims=True)
        p = jnp.exp2(s - m)
        l = jnp.sum(p, axis=-1, keepdims=True)
        vw = v_ref[0, 0, pl.ds(window_start(i), win_tokens), :]
        o = jnp.dot(p.astype(BF16), vw, preferred_element_type=F32) / l
        z = z_ref[0, i * sub_tokens:(i + 1) * sub_tokens, :].astype(F32)
        y_ref[i * sub_tokens:(i + 1) * sub_tokens, :] = (o * _silu(z)).astype(BF16)

    s_cur = scores(0)
    for i in range(n_sub):
        s_next = scores(i + 1) if i + 1 < n_sub else None
        finish(i, s_cur)
        s_cur = s_next


def _na_attention(proj, rp, *, bsz, t):
    n_blocks, m, hd = proj.shape
    assert n_blocks == 4 * B_HEADS and m == bsz * t and hd == LANES
    rows = t // GRID_W
    assert t % (Q_ROWS * GRID_W) == 0 and rows >= 2 * Q_ROWS and rows >= WIN_ROWS
    assert Q_ROWS // SUB_ROWS >= 4 and NA_ROWS + SUB_ROWS - 1 <= WIN_ROWS
    n_rb = rows // Q_ROWS
    blk = Q_ROWS * GRID_W
    proj4 = proj.reshape(n_blocks, bsz, t, hd)
    return pl.pallas_call(
        functools.partial(_na_kernel, rows=rows),
        grid=(B_HEADS, bsz, n_rb),
        in_specs=[
            pl.BlockSpec((1, blk, hd), lambda h, b, r: (h, b * n_rb + r, 0)),
            pl.BlockSpec((1, 1, t, hd), lambda h, b, r: (B_HEADS + h, b, 0, 0)),
            pl.BlockSpec((1, 1, t, hd), lambda h, b, r: (2 * B_HEADS + h, b, 0, 0)),
            pl.BlockSpec((1, blk, hd), lambda h, b, r: (3 * B_HEADS + h, b * n_rb + r, 0)),
            pl.BlockSpec((1, _PAIR_ROWS, LANES), lambda h, b, r: (h, 0, 0)),
        ],
        out_specs=pl.BlockSpec((blk, hd), lambda h, b, r: (b * n_rb + r, h)),
        out_shape=jax.ShapeDtypeStruct((m, B_HEADS * hd), BF16),
        scratch_shapes=[pltpu.VMEM((len(_BIAS_VARIANTS), SUB_ROWS * GRID_W, WIN_ROWS * GRID_W), F32)],
        compiler_params=pltpu.CompilerParams(
            dimension_semantics=("arbitrary", "arbitrary", "arbitrary"),
            vmem_limit_bytes=VMEM_LIMIT_BYTES),
        name="na_attention",
    )(proj, proj4, proj4, proj, rp)


def _pair_table(rpb):
    h, nr, nc = rpb.shape
    n_off = _PAIR_ROWS + 1
    first = -(NA_ROWS - 1) - _PAIR_MIN
    ext = jnp.zeros((h, n_off, GRID_W), F32)
    ext = ext.at[:, first:first + nr, :nc].set(rpb.astype(F32))
    return jnp.concatenate([ext[:, :-1], ext[:, 1:]], axis=-1)


def _out_norm_kernel(y_ref, w_ref, x_ref, g_ref, o_ref):
    acc = jnp.dot(y_ref[...], w_ref[...], preferred_element_type=F32)
    xn = x_ref[...] + acc
    inv = lax.rsqrt(jnp.mean(xn * xn, axis=-1, keepdims=True) + EPS)
    o_ref[...] = (xn * inv) * g_ref[...]


def _out_norm(y, w_out, x, g, *, tm):
    m, d = x.shape
    e = w_out.shape[0]
    assert m % tm == 0
    return pl.pallas_call(
        _out_norm_kernel,
        grid=(m // tm,),
        in_specs=[
            pl.BlockSpec((tm, e), lambda i: (i, 0)),
            pl.BlockSpec((e, d), lambda i: (0, 0), pipeline_mode=pl.Buffered(1)),
            pl.BlockSpec((tm, d), lambda i: (i, 0)),
            pl.BlockSpec((1, d), lambda i: (0, 0)),
        ],
        out_specs=pl.BlockSpec((tm, d), lambda i: (i, 0)),
        out_shape=jax.ShapeDtypeStruct((m, d), F32),
        compiler_params=pltpu.CompilerParams(
            dimension_semantics=("arbitrary",),
            vmem_limit_bytes=VMEM_LIMIT_BYTES),
        name="out_proj_final_norm",
    )(y, w_out, x, g.reshape(1, d))


def _trunk(x, norm_g, final_g, a_w_in, a_v_g, a_ws, a_bs, a_w_out, b_w_in, rp, b_w_out):
    bsz, t, d = x.shape
    x2 = x.reshape(bsz * t, d)
    e = a_w_out.shape[0]
    head_dim = e // B_HEADS
    proj_a = _norm_matmul(x2, norm_g[0], a_w_in, tm=1024, tn=1024)
    x2 = _gmlp_out(proj_a, a_v_g, a_ws, a_bs, a_w_out, x2, tm=256)
    proj_b = _norm_matmul(x2, norm_g[1], b_w_in, tm=1024, tn=1024,
                          n_scaled_tiles=e // 1024, scale=head_dim ** -0.5 * LOG2E, col_major=True)
    y = _na_attention(proj_b, rp, bsz=bsz, t=t)
    out = _out_norm(y, b_w_out, x2, final_g, tm=512)
    return out.reshape(bsz, t, d)


def kernel(x_prompt, x_sample, norm_g, final_g, a_w_in, a_v_g, a_ws, a_bs, a_w_out, b_w_in, b_rpb,
           b_w_out):
    assert norm_g.shape[0] == 2 and a_w_in.shape[0] == 1 and b_w_in.shape[0] == 1
    params = (norm_g, final_g, a_w_in[0].astype(BF16), a_v_g[0], a_ws[0].astype(BF16), a_bs[0],
              a_w_out[0].astype(BF16), b_w_in[0].astype(BF16), _pair_table(b_rpb[0]),
              b_w_out[0].astype(BF16))
    return (_trunk(x_prompt, *params), _trunk(x_sample, *params))
```

```python
import functools

import jax
import jax.numpy as jnp
from jax import lax
from jax.experimental import pallas as pl
from jax.experimental.pallas import tpu as pltpu

F32 = jnp.float32
BF16 = jnp.bfloat16

EXPAND = 2
GRID_W = 64
CHUNK = 128
A_GROUPS = 16
B_HEADS = 32
NA_ROWS = 8
NA_COLS = 16
EPS = 1e-6

LANES = 128
VMEM_LIMIT_BYTES = 56 * 1024 * 1024

Q_ROWS = 32
SUB_ROWS = 2
WIN_ROWS = 10
NEG = -1e30
LOG2E = 1.4426950408889634


def _gelu_tanh(x):
    c = 0.7978845608028654
    a = -2.0 * c * 0.044715 * LOG2E
    b = -2.0 * c * LOG2E
    return x / (1.0 + jnp.exp2(x * (a * (x * x) + b)))


def _silu(x):
    return x / (1.0 + jnp.exp2(-LOG2E * x))


def _norm_matmul_kernel(x_ref, g_ref, w_ref, o_ref, h_ref, *, n_scaled_tiles, scale, col_major):
    j = pl.program_id(1)

    @pl.when(j == 0)
    def _():
        xf = x_ref[...]
        inv = lax.rsqrt(jnp.mean(xf * xf, axis=-1, keepdims=True) + EPS)
        h_ref[...] = ((xf * inv) * g_ref[...]).astype(BF16)

    acc = jnp.dot(h_ref[...], w_ref[...], preferred_element_type=F32)
    if n_scaled_tiles:
        acc = acc * jnp.where(j < n_scaled_tiles, scale, 1.0).astype(F32)
    if col_major:
        for c in range(o_ref.shape[0]):
            o_ref[c] = acc[:, c * LANES:(c + 1) * LANES].astype(o_ref.dtype)
    else:
        o_ref[...] = acc.astype(o_ref.dtype)


def _norm_matmul(x, g, w, *, tm, tn, n_scaled_tiles=0, scale=1.0, col_major=False):
    m, d = x.shape
    n = w.shape[1]
    assert m % tm == 0 and n % tn == 0 and tn % LANES == 0
    if col_major:
        out_shape = jax.ShapeDtypeStruct((n // LANES, m, LANES), BF16)
        out_spec = pl.BlockSpec((tn // LANES, tm, LANES), lambda i, j: (j, i, 0))
    else:
        out_shape = jax.ShapeDtypeStruct((m, n), BF16)
        out_spec = pl.BlockSpec((tm, tn), lambda i, j: (i, j))
    return pl.pallas_call(
        functools.partial(_norm_matmul_kernel, n_scaled_tiles=n_scaled_tiles, scale=scale,
                          col_major=col_major),
        grid=(m // tm, n // tn),
        in_specs=[
            pl.BlockSpec((tm, d), lambda i, j: (i, 0)),
            pl.BlockSpec((1, d), lambda i, j: (0, 0)),
            pl.BlockSpec((d, tn), lambda i, j: (0, j)),
        ],
        out_specs=out_spec,
        out_shape=out_shape,
        scratch_shapes=[pltpu.VMEM((tm, d), BF16)],
        compiler_params=pltpu.CompilerParams(
            dimension_semantics=("arbitrary", "arbitrary"),
            vmem_limit_bytes=VMEM_LIMIT_BYTES),
        name="norm_in_proj",
    )(x, g.reshape(1, d), w)


def _gmlp_out_kernel(u_ref, v_ref, z_ref, vg_ref, ws_ref, bs_ref, wout_ref, x_ref, o_ref,
                     gv_ref, y_ref):
    s = pl.program_id(0)
    tm, e = u_ref.shape
    d = o_ref.shape[1]
    gdim = e // A_GROUPS
    cur = s % 2
    prev = 1 - cur
    n_out_chunks = 8
    out_cols = d // n_out_chunks
    groups_per_chunk = 2 * A_GROUPS // n_out_chunks

    @pl.when(s == 0)
    def _():
        y_ref[1] = jnp.zeros((tm, e), BF16)

    def out_chunk(c):
        cols = slice(c * out_cols, (c + 1) * out_cols)
        acc = jnp.dot(y_ref[prev], wout_ref[:, cols], preferred_element_type=F32)
        o_ref[:, cols] = x_ref[:, cols] + acc

    def gelu_v(g):
        cols = slice(g * gdim, (g + 1) * gdim)
        gv = _gelu_tanh(v_ref[:, cols].astype(F32))
        gv_ref[:, cols] = gv
        return jnp.sum(gv * gv, axis=-1, keepdims=True)

    def gate(g, inv):
        cols = slice(g * gdim, (g + 1) * gdim)
        vn = ((gv_ref[:, cols] * inv) * vg_ref[:, cols]).astype(BF16)
        for c in range(tm // CHUNK):
            rows = slice(c * CHUNK, (c + 1) * CHUNK)
            sg = jnp.dot(ws_ref[g], vn[rows], preferred_element_type=F32) + bs_ref[g]
            u = _gelu_tanh(u_ref[rows, cols].astype(F32))
            z = z_ref[rows, cols].astype(F32)
            y_ref[cur, rows, cols] = ((u * sg) * _silu(z)).astype(BF16)

    ssq = jnp.zeros((tm, 1), F32)
    inv = None
    for unit in range(2 * A_GROUPS):
        if unit % groups_per_chunk == 0:
            out_chunk(unit // groups_per_chunk)
        if unit < A_GROUPS:
            ssq = ssq + gelu_v(unit)
        else:
            if inv is None:
                inv = lax.rsqrt(ssq * (1.0 / e) + EPS)
            gate(unit - A_GROUPS, inv)


def _gmlp_out(proj, v_g, w_s, b_s, w_out, x, *, tm):
    m, d = x.shape
    e = w_out.shape[0]
    assert m % tm == 0 and tm % CHUNK == 0
    n_tiles = m // tm
    gdim = e // A_GROUPS
    cur_tile = lambda s: jnp.minimum(s, n_tiles - 1)
    prev_tile = lambda s: jnp.maximum(s - 1, 0)
    return pl.pallas_call(
        _gmlp_out_kernel,
        grid=(n_tiles + 1,),
        in_specs=[
            pl.BlockSpec((tm, e), lambda s: (cur_tile(s), 0)),
            pl.BlockSpec((tm, e), lambda s: (cur_tile(s), 1)),
            pl.BlockSpec((tm, e), lambda s: (cur_tile(s), 2)),
            pl.BlockSpec((1, e), lambda s: (0, 0)),
            pl.BlockSpec((A_GROUPS, CHUNK, CHUNK), lambda s: (0, 0, 0)),
            pl.BlockSpec((A_GROUPS, CHUNK, gdim), lambda s: (0, 0, 0)),
            pl.BlockSpec((e, d), lambda s: (0, 0), pipeline_mode=pl.Buffered(1)),
            pl.BlockSpec((tm, d), lambda s: (prev_tile(s), 0)),
        ],
        out_specs=pl.BlockSpec((tm, d), lambda s: (prev_tile(s), 0)),
        out_shape=jax.ShapeDtypeStruct((m, d), F32),
        scratch_shapes=[pltpu.VMEM((tm, e), F32), pltpu.VMEM((2, tm, e), BF16)],
        compiler_params=pltpu.CompilerParams(
            dimension_semantics=("arbitrary",),
            vmem_limit_bytes=VMEM_LIMIT_BYTES),
        name="gmlp_gate_out_proj",
    )(proj, proj, proj, v_g.reshape(1, e), w_s,
      jnp.broadcast_to(b_s[:, :, None], (A_GROUPS, CHUNK, gdim)), w_out, x)


def _window_variants():
    rows = 4 * (NA_ROWS + WIN_ROWS)

    def at(r_a):
        r0 = [min(max(r_a + a - NA_ROWS // 2, 0), rows - NA_ROWS) for a in range(SUB_ROWS)]
        ws = min(r0[0], rows - WIN_ROWS)
        return (ws - r_a, tuple(r - ws for r in r0))

    interior = at(rows // 2 // SUB_ROWS * SUB_ROWS)
    tops, bottoms = [], []
    while at(len(tops) * SUB_ROWS) != interior:
        tops.append(at(len(tops) * SUB_ROWS))
    while at(rows - (len(bottoms) + 1) * SUB_ROWS) != interior:
        bottoms.insert(0, at(rows - (len(bottoms) + 1) * SUB_ROWS))
    return tuple(tops) + (interior,) + tuple(bottoms), len(tops), len(bottoms)


_BIAS_VARIANTS, _N_TOP, _N_BOTTOM = _window_variants()
_PAIR_MIN = min(delta for delta, _ in _BIAS_VARIANTS) - (SUB_ROWS - 1)
_PAIR_ROWS = 24
assert max(delta for delta, _ in _BIAS_VARIANTS) + WIN_ROWS - 2 - _PAIR_MIN < _PAIR_ROWS
_NT_DIMS = (((1,), (1,)), ((), ()))


def _build_bias(rp_ref, bias_ref):
    c = lax.broadcasted_iota(jnp.int32, (GRID_W, LANES), 0)
    lane = lax.broadcasted_iota(jnp.int32, (GRID_W, LANES), 1)
    kc = lane & (GRID_W - 1)
    cs = jnp.clip(c - NA_COLS // 2, 0, GRID_W - NA_COLS)
    col_ok = (kc >= cs) & (kc < cs + NA_COLS)
    left = lane < GRID_W
    pair_cache = {}

    def pair(d):
        if d not in pair_cache:
            row = rp_ref[0, d - _PAIR_MIN:d - _PAIR_MIN + 1, :]
            x = jnp.broadcast_to(row, (GRID_W, LANES))
            x = pltpu.roll(x, LANES - (NA_COLS - 1), 1, stride=1, stride_axis=0)
            pair_cache[d] = x * LOG2E
        return pair_cache[d]

    for v, (delta, los) in enumerate(_BIAS_VARIANTS):
        for a in range(SUB_ROWS):
            lo = los[a]
            for p in range(WIN_ROWS // 2):
                ok_l = lo <= 2 * p <= lo + NA_ROWS - 1
                ok_r = lo <= 2 * p + 1 <= lo + NA_ROWS - 1
                if ok_l and ok_r:
                    mask = col_ok
                elif ok_l:
                    mask = col_ok & left
                elif ok_r:
                    mask = col_ok & jnp.logical_not(left)
                else:
                    mask = None
                if mask is None:
                    blk = jnp.full((GRID_W, LANES), NEG, F32)
                else:
                    blk = jnp.where(mask, pair(delta + 2 * p - a), NEG)
                bias_ref[v, a * GRID_W:(a + 1) * GRID_W, p * LANES:(p + 1) * LANES] = blk


def _na_kernel(q_ref, kv_hbm, z_ref, rp_ref, y_ref, bias_ref, kv_ref, kv_sem, *, rows):
    h = pl.program_id(0)
    b = pl.program_id(1)
    rb = pl.program_id(2)
    n_heads = pl.num_programs(0)
    bsz = pl.num_programs(1)
    n_rb = pl.num_programs(2)
    sub_tokens = SUB_ROWS * GRID_W
    win_tokens = WIN_ROWS * GRID_W
    n_sub = Q_ROWS // SUB_ROWS
    slot = (h * bsz + b) % 2

    def kv_copies(head, batch, slot_):
        return [pltpu.make_async_copy(kv_hbm.at[(1 + kv) * n_heads + head, batch],
                                      kv_ref.at[slot_, kv], kv_sem.at[slot_, kv])
                for kv in range(2)]

    @pl.when(rb == 0)
    def _():
        @pl.when((h == 0) & (b == 0))
        def _():
            for cp in kv_copies(h, b, slot):
                cp.start()

        for cp in kv_copies(h, b, slot):
            cp.wait()

        last_batch = b == bsz - 1

        @pl.when(jnp.logical_not(last_batch & (h == n_heads - 1)))
        def _():
            next_h = jnp.where(last_batch, h + 1, h)
            next_b = jnp.where(last_batch, 0, b + 1)
            for cp in kv_copies(next_h, next_b, 1 - slot):
                cp.start()

    @pl.when((b == 0) & (rb == 0))
    def _():
        _build_bias(rp_ref, bias_ref)

    def window_start(i):
        r_a = rb * Q_ROWS + i * SUB_ROWS
        ws = jnp.minimum(jnp.maximum(r_a - NA_ROWS // 2, 0), rows - WIN_ROWS)
        return pl.multiple_of(ws * GRID_W, GRID_W)

    def variant(i):
        v = _N_TOP
        if i < _N_TOP:
            v = jnp.where(rb == 0, i, v)
        if i >= n_sub - _N_BOTTOM:
            v = jnp.where(rb == n_rb - 1, _N_TOP + 1 + i - (n_sub - _N_BOTTOM), v)
        return v

    def scores(i):
        q = q_ref[0, i * sub_tokens:(i + 1) * sub_tokens, :]
        kw = kv_ref[slot, 0, pl.ds(window_start(i), win_tokens), :]
        s = lax.dot_general(q, kw, _NT_DIMS, preferred_element_type=F32)
        return s + bias_ref[variant(i)]

    def finish(i, s):
        m = jnp.max(s, axis=-1, keepdims=True)
        p = jnp.exp2(s - m)
        l = jnp.sum(p, axis=-1, keepdims=True)
        vw = kv_ref[slot, 1, pl.ds(window_start(i), win_tokens), :]
        o = jnp.dot(p.astype(BF16), vw, preferred_element_type=F32) / l
        z = z_ref[0, i * sub_tokens:(i + 1) * sub_tokens, :].astype(F32)
        y_ref[i * sub_tokens:(i + 1) * sub_tokens, :] = (o * _silu(z)).astype(BF16)

    s_cur = scores(0)
    for i in range(n_sub):
        s_next = scores(i + 1) if i + 1 < n_sub else None
        finish(i, s_cur)
        s_cur = s_next


def _na_attention(proj, rp, *, bsz, t):
    n_blocks, m, hd = proj.shape
    assert n_blocks == 4 * B_HEADS and m == bsz * t and hd == LANES
    rows = t // GRID_W
    assert t % (Q_ROWS * GRID_W) == 0 and rows >= 2 * Q_ROWS and rows >= WIN_ROWS
    assert Q_ROWS // SUB_ROWS >= _N_TOP + _N_BOTTOM and NA_ROWS + SUB_ROWS - 1 <= WIN_ROWS
    n_rb = rows // Q_ROWS
    blk = Q_ROWS * GRID_W
    proj4 = proj.reshape(n_blocks, bsz, t, hd)
    return pl.pallas_call(
        functools.partial(_na_kernel, rows=rows),
        grid=(B_HEADS, bsz, n_rb),
        in_specs=[
            pl.BlockSpec((1, blk, hd), lambda h, b, r: (h, b * n_rb + r, 0)),
            pl.BlockSpec(memory_space=pl.ANY),
            pl.BlockSpec((1, blk, hd), lambda h, b, r: (3 * B_HEADS + h, b * n_rb + r, 0)),
            pl.BlockSpec((1, _PAIR_ROWS, LANES), lambda h, b, r: (h, 0, 0)),
        ],
        out_specs=pl.BlockSpec((blk, hd), lambda h, b, r: (b * n_rb + r, h)),
        out_shape=jax.ShapeDtypeStruct((m, B_HEADS * hd), BF16),
        scratch_shapes=[
            pltpu.VMEM((len(_BIAS_VARIANTS), SUB_ROWS * GRID_W, WIN_ROWS * GRID_W), F32),
            pltpu.VMEM((2, 2, t, hd), BF16),
            pltpu.SemaphoreType.DMA((2, 2)),
        ],
        compiler_params=pltpu.CompilerParams(
            dimension_semantics=("arbitrary", "arbitrary", "arbitrary"),
            vmem_limit_bytes=VMEM_LIMIT_BYTES),
        name="na_attention",
    )(proj, proj4, proj, rp)


def _pair_table(rpb):
    h, nr, nc = rpb.shape
    n_off = _PAIR_ROWS + 1
    first = -(NA_ROWS - 1) - _PAIR_MIN
    ext = jnp.zeros((h, n_off, GRID_W), F32)
    ext = ext.at[:, first:first + nr, :nc].set(rpb.astype(F32))
    return jnp.concatenate([ext[:, :-1], ext[:, 1:]], axis=-1)


def _out_norm_kernel(y_ref, w_ref, x_ref, g_ref, o_ref):
    acc = jnp.dot(y_ref[...], w_ref[...], preferred_element_type=F32)
    xn = x_ref[...] + acc
    inv = lax.rsqrt(jnp.mean(xn * xn, axis=-1, keepdims=True) + EPS)
    o_ref[...] = (xn * inv) * g_ref[...]


def _out_norm(y, w_out, x, g, *, tm):
    m, d = x.shape
    e = w_out.shape[0]
    assert m % tm == 0
    return pl.pallas_call(
        _out_norm_kernel,
        grid=(m // tm,),
        in_specs=[
            pl.BlockSpec((tm, e), lambda i: (i, 0)),
            pl.BlockSpec((e, d), lambda i: (0, 0), pipeline_mode=pl.Buffered(1)),
            pl.BlockSpec((tm, d), lambda i: (i, 0)),
            pl.BlockSpec((1, d), lambda i: (0, 0)),
        ],
        out_specs=pl.BlockSpec((tm, d), lambda i: (i, 0)),
        out_shape=jax.ShapeDtypeStruct((m, d), F32),
        compiler_params=pltpu.CompilerParams(
            dimension_semantics=("arbitrary",),
            vmem_limit_bytes=VMEM_LIMIT_BYTES),
        name="out_proj_final_norm",
    )(y, w_out, x, g.reshape(1, d))


def _trunk(x, norm_g, final_g, a_w_in, a_v_g, a_ws, a_bs, a_w_out, b_w_in, rp, b_w_out):
    bsz, t, d = x.shape
    x2 = x.reshape(bsz * t, d)
    e = a_w_out.shape[0]
    head_dim = e // B_HEADS
    proj_a = _norm_matmul(x2, norm_g[0], a_w_in, tm=1024, tn=2048)
    x2 = _gmlp_out(proj_a, a_v_g, a_ws, a_bs, a_w_out, x2, tm=256)
    proj_b = _norm_matmul(x2, norm_g[1], b_w_in, tm=1024, tn=2048,
                          n_scaled_tiles=e // 2048, scale=head_dim ** -0.5 * LOG2E, col_major=True)
    y = _na_attention(proj_b, rp, bsz=bsz, t=t)
    out = _out_norm(y, b_w_out, x2, final_g, tm=512)
    return out.reshape(bsz, t, d)


def kernel(x_prompt, x_sample, norm_g, final_g, a_w_in, a_v_g, a_ws, a_bs, a_w_out, b_w_in, b_rpb,
           b_w_out):
    assert norm_g.shape[0] == 2 and a_w_in.shape[0] == 1 and b_w_in.shape[0] == 1
    params = (norm_g, final_g, a_w_in[0].astype(BF16), a_v_g[0], a_ws[0].astype(BF16), a_bs[0],
              a_w_out[0].astype(BF16), b_w_in[0].astype(BF16), _pair_table(b_rpb[0]),
              b_w_out[0].astype(BF16))
    return (_trunk(x_prompt, *params), _trunk(x_sample, *params))
```

```python
import functools

import jax
import jax.numpy as jnp
from jax import lax
from jax.experimental import pallas as pl
from jax.experimental.pallas import tpu as pltpu

F32 = jnp.float32
BF16 = jnp.bfloat16

EXPAND = 2
GRID_W = 64
CHUNK = 128
A_GROUPS = 16
B_HEADS = 32
NA_ROWS = 8
NA_COLS = 16
EPS = 1e-6

LANES = 128
VMEM_LIMIT_BYTES = 56 * 1024 * 1024

Q_ROWS = 32
SUB_ROWS = 2
WIN_ROWS = 10
NEG = -1e30
LOG2E = 1.4426950408889634


def _gelu_tanh(x):
    c = 0.7978845608028654
    a = -2.0 * c * 0.044715 * LOG2E
    b = -2.0 * c * LOG2E
    return x / (1.0 + jnp.exp2(x * (a * (x * x) + b)))


def _silu(x):
    return x / (1.0 + jnp.exp2(-LOG2E * x))


def _epilogue(acc, kind):
    if kind == "gelu":
        return _gelu_tanh(acc)
    if kind == "silu":
        return _silu(acc)
    if kind is None:
        return acc
    return acc * kind


def _norm_matmul_kernel(x_ref, g_ref, w_ref, o_ref, h_ref, *, epilogues, col_major):
    j = pl.program_id(1)

    @pl.when(j == 0)
    def _():
        xf = x_ref[...]
        inv = lax.rsqrt(jnp.mean(xf * xf, axis=-1, keepdims=True) + EPS)
        h_ref[...] = ((xf * inv) * g_ref[...]).astype(BF16)

    for lo, hi, kind in epilogues:
        @pl.when((j >= lo) & (j < hi))
        def _(kind=kind):
            acc = jnp.dot(h_ref[...], w_ref[...], preferred_element_type=F32)
            acc = _epilogue(acc, kind)
            if col_major:
                for c in range(o_ref.shape[0]):
                    o_ref[c] = acc[:, c * LANES:(c + 1) * LANES].astype(o_ref.dtype)
            else:
                o_ref[...] = acc.astype(o_ref.dtype)


def _norm_matmul(x, g, w, *, tm, tn, epilogues, col_major=False):
    m, d = x.shape
    n = w.shape[1]
    assert m % tm == 0 and n % tn == 0 and tn % LANES == 0
    tile_epilogues, lo = [], 0
    for n_cols, kind in epilogues:
        assert n_cols % tn == 0
        tile_epilogues.append((lo, lo + n_cols // tn, kind))
        lo += n_cols // tn
    assert lo == n // tn
    if col_major:
        out_shape = jax.ShapeDtypeStruct((n // LANES, m, LANES), BF16)
        out_spec = pl.BlockSpec((tn // LANES, tm, LANES), lambda i, j: (j, i, 0))
    else:
        out_shape = jax.ShapeDtypeStruct((m, n), BF16)
        out_spec = pl.BlockSpec((tm, tn), lambda i, j: (i, j))
    return pl.pallas_call(
        functools.partial(_norm_matmul_kernel, epilogues=tuple(tile_epilogues),
                          col_major=col_major),
        grid=(m // tm, n // tn),
        in_specs=[
            pl.BlockSpec((tm, d), lambda i, j: (i, 0)),
            pl.BlockSpec((1, d), lambda i, j: (0, 0)),
            pl.BlockSpec((d, tn), lambda i, j: (0, j)),
        ],
        out_specs=out_spec,
        out_shape=out_shape,
        scratch_shapes=[pltpu.VMEM((tm, d), BF16)],
        compiler_params=pltpu.CompilerParams(
            dimension_semantics=("arbitrary", "arbitrary"),
            vmem_limit_bytes=VMEM_LIMIT_BYTES),
        name="norm_in_proj",
    )(x, g.reshape(1, d), w)


def _gmlp_out_kernel(u_ref, v_ref, z_ref, vg_ref, ws_ref, bs_ref, wout_ref, x_ref, o_ref, y_ref):
    s = pl.program_id(0)
    tm, e = u_ref.shape
    d = o_ref.shape[1]
    gdim = e // A_GROUPS
    cur = s % 2
    prev = 1 - cur
    n_out_chunks = 8
    out_cols = d // n_out_chunks
    groups_per_chunk = 2 * A_GROUPS // n_out_chunks

    @pl.when(s == 0)
    def _():
        y_ref[1] = jnp.zeros((tm, e), BF16)

    def out_chunk(c):
        cols = slice(c * out_cols, (c + 1) * out_cols)
        acc = jnp.dot(y_ref[prev], wout_ref[:, cols], preferred_element_type=F32)
        o_ref[:, cols] = x_ref[:, cols] + acc

    def sum_squares(g):
        gv = v_ref[:, g * gdim:(g + 1) * gdim].astype(F32)
        return jnp.sum(gv * gv, axis=-1, keepdims=True)

    def gate(g, inv):
        cols = slice(g * gdim, (g + 1) * gdim)
        vn = ((v_ref[:, cols].astype(F32) * inv) * vg_ref[:, cols]).astype(BF16)
        for c in range(tm // CHUNK):
            rows = slice(c * CHUNK, (c + 1) * CHUNK)
            sg = jnp.dot(ws_ref[g], vn[rows], preferred_element_type=F32) + bs_ref[g]
            u = u_ref[rows, cols].astype(F32)
            z = z_ref[rows, cols].astype(F32)
            y_ref[cur, rows, cols] = ((u * sg) * z).astype(BF16)

    ssq = jnp.zeros((tm, 1), F32)
    inv = None
    for unit in range(2 * A_GROUPS):
        if unit % groups_per_chunk == 0:
            out_chunk(unit // groups_per_chunk)
        if unit < A_GROUPS:
            ssq = ssq + sum_squares(unit)
        else:
            if inv is None:
                inv = lax.rsqrt(ssq * (1.0 / e) + EPS)
            gate(unit - A_GROUPS, inv)


def _gmlp_out(proj, v_g, w_s, b_s, w_out, x, *, tm):
    m, d = x.shape
    e = w_out.shape[0]
    assert m % tm == 0 and tm % CHUNK == 0
    n_tiles = m // tm
    gdim = e // A_GROUPS
    cur_tile = lambda s: jnp.minimum(s, n_tiles - 1)
    prev_tile = lambda s: jnp.maximum(s - 1, 0)
    return pl.pallas_call(
        _gmlp_out_kernel,
        grid=(n_tiles + 1,),
        in_specs=[
            pl.BlockSpec((tm, e), lambda s: (cur_tile(s), 0)),
            pl.BlockSpec((tm, e), lambda s: (cur_tile(s), 1)),
            pl.BlockSpec((tm, e), lambda s: (cur_tile(s), 2)),
            pl.BlockSpec((1, e), lambda s: (0, 0)),
            pl.BlockSpec((A_GROUPS, CHUNK, CHUNK), lambda s: (0, 0, 0)),
            pl.BlockSpec((A_GROUPS, CHUNK, gdim), lambda s: (0, 0, 0)),
            pl.BlockSpec((e, d), lambda s: (0, 0), pipeline_mode=pl.Buffered(1)),
            pl.BlockSpec((tm, d), lambda s: (prev_tile(s), 0)),
        ],
        out_specs=pl.BlockSpec((tm, d), lambda s: (prev_tile(s), 0)),
        out_shape=jax.ShapeDtypeStruct((m, d), F32),
        scratch_shapes=[pltpu.VMEM((2, tm, e), BF16)],
        compiler_params=pltpu.CompilerParams(
            dimension_semantics=("arbitrary",),
            vmem_limit_bytes=VMEM_LIMIT_BYTES),
        name="gmlp_gate_out_proj",
    )(proj, proj, proj, v_g.reshape(1, e), w_s,
      jnp.broadcast_to(b_s[:, :, None], (A_GROUPS, CHUNK, gdim)), w_out, x)


def _window_variants():
    rows = 4 * (NA_ROWS + WIN_ROWS)

    def at(r_a):
        r0 = [min(max(r_a + a - NA_ROWS // 2, 0), rows - NA_ROWS) for a in range(SUB_ROWS)]
        ws = min(r0[0], rows - WIN_ROWS)
        return (ws - r_a, tuple(r - ws for r in r0))

    interior = at(rows // 2 // SUB_ROWS * SUB_ROWS)
    tops, bottoms = [], []
    while at(len(tops) * SUB_ROWS) != interior:
        tops.append(at(len(tops) * SUB_ROWS))
    while at(rows - (len(bottoms) + 1) * SUB_ROWS) != interior:
        bottoms.insert(0, at(rows - (len(bottoms) + 1) * SUB_ROWS))
    return tuple(tops) + (interior,) + tuple(bottoms), len(tops), len(bottoms)


_BIAS_VARIANTS, _N_TOP, _N_BOTTOM = _window_variants()
_PAIR_MIN = min(delta for delta, _ in _BIAS_VARIANTS) - (SUB_ROWS - 1)
_PAIR_ROWS = 24
assert max(delta for delta, _ in _BIAS_VARIANTS) + WIN_ROWS - 2 - _PAIR_MIN < _PAIR_ROWS
_NT_DIMS = (((1,), (1,)), ((), ()))


def _build_bias(rp_ref, bias_ref):
    c = lax.broadcasted_iota(jnp.int32, (GRID_W, LANES), 0)
    lane = lax.broadcasted_iota(jnp.int32, (GRID_W, LANES), 1)
    kc = lane & (GRID_W - 1)
    cs = jnp.clip(c - NA_COLS // 2, 0, GRID_W - NA_COLS)
    col_ok = (kc >= cs) & (kc < cs + NA_COLS)
    left = lane < GRID_W
    pair_cache = {}

    def pair(d):
        if d not in pair_cache:
            row = rp_ref[0, d - _PAIR_MIN:d - _PAIR_MIN + 1, :]
            x = jnp.broadcast_to(row, (GRID_W, LANES))
            x = pltpu.roll(x, LANES - (NA_COLS - 1), 1, stride=1, stride_axis=0)
            pair_cache[d] = x * LOG2E
        return pair_cache[d]

    for v, (delta, los) in enumerate(_BIAS_VARIANTS):
        for a in range(SUB_ROWS):
            lo = los[a]
            for p in range(WIN_ROWS // 2):
                ok_l = lo <= 2 * p <= lo + NA_ROWS - 1
                ok_r = lo <= 2 * p + 1 <= lo + NA_ROWS - 1
                if ok_l and ok_r:
                    mask = col_ok
                elif ok_l:
                    mask = col_ok & left
                elif ok_r:
                    mask = col_ok & jnp.logical_not(left)
                else:
                    mask = None
                if mask is None:
                    blk = jnp.full((GRID_W, LANES), NEG, F32)
                else:
                    blk = jnp.where(mask, pair(delta + 2 * p - a), NEG)
                bias_ref[v, a * GRID_W:(a + 1) * GRID_W, p * LANES:(p + 1) * LANES] = blk


def _na_kernel(q_ref, kv_hbm, z_ref, rp_ref, y_ref, bias_ref, kv_ref, kv_sem, *, rows):
    h = pl.program_id(0)
    b = pl.program_id(1)
    rb = pl.program_id(2)
    n_heads = pl.num_programs(0)
    bsz = pl.num_programs(1)
    n_rb = pl.num_programs(2)
    sub_tokens = SUB_ROWS * GRID_W
    win_tokens = WIN_ROWS * GRID_W
    n_sub = Q_ROWS // SUB_ROWS
    slot = (h * bsz + b) % 2

    def kv_copies(head, batch, slot_):
        return [pltpu.make_async_copy(kv_hbm.at[(1 + kv) * n_heads + head, batch],
                                      kv_ref.at[slot_, kv], kv_sem.at[slot_, kv])
                for kv in range(2)]

    @pl.when(rb == 0)
    def _():
        @pl.when((h == 0) & (b == 0))
        def _():
            for cp in kv_copies(h, b, slot):
                cp.start()

        for cp in kv_copies(h, b, slot):
            cp.wait()

        last_batch = b == bsz - 1

        @pl.when(jnp.logical_not(last_batch & (h == n_heads - 1)))
        def _():
            next_h = jnp.where(last_batch, h + 1, h)
            next_b = jnp.where(last_batch, 0, b + 1)
            for cp in kv_copies(next_h, next_b, 1 - slot):
                cp.start()

    @pl.when((b == 0) & (rb == 0))
    def _():
        _build_bias(rp_ref, bias_ref)

    def window_start(i):
        r_a = rb * Q_ROWS + i * SUB_ROWS
        ws = jnp.minimum(jnp.maximum(r_a - NA_ROWS // 2, 0), rows - WIN_ROWS)
        return pl.multiple_of(ws * GRID_W, GRID_W)

    def variant(i):
        v = _N_TOP
        if i < _N_TOP:
            v = jnp.where(rb == 0, i, v)
        if i >= n_sub - _N_BOTTOM:
            v = jnp.where(rb == n_rb - 1, _N_TOP + 1 + i - (n_sub - _N_BOTTOM), v)
        return v

    def scores(i):
        q = q_ref[0, i * sub_tokens:(i + 1) * sub_tokens, :]
        kw = kv_ref[slot, 0, pl.ds(window_start(i), win_tokens), :]
        s = lax.dot_general(q, kw, _NT_DIMS, preferred_element_type=F32)
        return s + bias_ref[variant(i)]

    def finish(i, s):
        m = jnp.max(s, axis=-1, keepdims=True)
        p = jnp.exp2(s - m)
        l = jnp.sum(p, axis=-1, keepdims=True)
        vw = kv_ref[slot, 1, pl.ds(window_start(i), win_tokens), :]
        o = jnp.dot(p.astype(BF16), vw, preferred_element_type=F32) / l
        z = z_ref[0, i * sub_tokens:(i + 1) * sub_tokens, :].astype(F32)
        y_ref[0, i * sub_tokens:(i + 1) * sub_tokens, :] = (o * z).astype(BF16)

    s_cur = scores(0)
    for i in range(n_sub):
        s_next = scores(i + 1) if i + 1 < n_sub else None
        finish(i, s_cur)
        s_cur = s_next


def _na_attention(proj, rp, *, bsz, t):
    n_blocks, m, hd = proj.shape
    assert n_blocks == 4 * B_HEADS and m == bsz * t and hd == LANES
    rows = t // GRID_W
    assert t % (Q_ROWS * GRID_W) == 0 and rows >= 2 * Q_ROWS and rows >= WIN_ROWS
    assert Q_ROWS // SUB_ROWS >= _N_TOP + _N_BOTTOM and NA_ROWS + SUB_ROWS - 1 <= WIN_ROWS
    n_rb = rows // Q_ROWS
    blk = Q_ROWS * GRID_W
    proj4 = proj.reshape(n_blocks, bsz, t, hd)
    return pl.pallas_call(
        functools.partial(_na_kernel, rows=rows),
        grid=(B_HEADS, bsz, n_rb),
        in_specs=[
            pl.BlockSpec((1, blk, hd), lambda h, b, r: (h, b * n_rb + r, 0)),
            pl.BlockSpec(memory_space=pl.ANY),
            pl.BlockSpec((1, blk, hd), lambda h, b, r: (3 * B_HEADS + h, b * n_rb + r, 0)),
            pl.BlockSpec((1, _PAIR_ROWS, LANES), lambda h, b, r: (h, 0, 0)),
        ],
        out_specs=pl.BlockSpec((1, blk, hd), lambda h, b, r: (h, b * n_rb + r, 0)),
        out_shape=jax.ShapeDtypeStruct((B_HEADS, m, hd), BF16),
        scratch_shapes=[
            pltpu.VMEM((len(_BIAS_VARIANTS), SUB_ROWS * GRID_W, WIN_ROWS * GRID_W), F32),
            pltpu.VMEM((2, 2, t, hd), BF16),
            pltpu.SemaphoreType.DMA((2, 2)),
        ],
        compiler_params=pltpu.CompilerParams(
            dimension_semantics=("arbitrary", "arbitrary", "arbitrary"),
            vmem_limit_bytes=VMEM_LIMIT_BYTES),
        name="na_attention",
    )(proj, proj4, proj, rp)


def _pair_table(rpb):
    h, nr, nc = rpb.shape
    n_off = _PAIR_ROWS + 1
    first = -(NA_ROWS - 1) - _PAIR_MIN
    ext = jnp.zeros((h, n_off, GRID_W), F32)
    ext = ext.at[:, first:first + nr, :nc].set(rpb.astype(F32))
    return jnp.concatenate([ext[:, :-1], ext[:, 1:]], axis=-1)


def _out_norm_kernel(y_ref, w_ref, x_ref, g_ref, o_ref):
    y = jnp.concatenate([y_ref[h] for h in range(y_ref.shape[0])], axis=1)
    acc = jnp.dot(y, w_ref[...], preferred_element_type=F32)
    xn = x_ref[...] + acc
    inv = lax.rsqrt(jnp.mean(xn * xn, axis=-1, keepdims=True) + EPS)
    o_ref[...] = (xn * inv) * g_ref[...]


def _out_norm(y, w_out, x, g, *, tm):
    m, d = x.shape
    e = w_out.shape[0]
    n_heads, _, hd = y.shape
    assert m % tm == 0 and n_heads * hd == e
    return pl.pallas_call(
        _out_norm_kernel,
        grid=(m // tm,),
        in_specs=[
            pl.BlockSpec((n_heads, tm, hd), lambda i: (0, i, 0)),
            pl.BlockSpec((e, d), lambda i: (0, 0), pipeline_mode=pl.Buffered(1)),
            pl.BlockSpec((tm, d), lambda i: (i, 0)),
            pl.BlockSpec((1, d), lambda i: (0, 0)),
        ],
        out_specs=pl.BlockSpec((tm, d), lambda i: (i, 0)),
        out_shape=jax.ShapeDtypeStruct((m, d), F32),
        compiler_params=pltpu.CompilerParams(
            dimension_semantics=("arbitrary",),
            vmem_limit_bytes=VMEM_LIMIT_BYTES),
        name="out_proj_final_norm",
    )(y, w_out, x, g.reshape(1, d))


def _trunk(x, norm_g, final_g, a_w_in, a_v_g, a_ws, a_bs, a_w_out, b_w_in, rp, b_w_out):
    bsz, t, d = x.shape
    x2 = x.reshape(bsz * t, d)
    e = a_w_out.shape[0]
    head_dim = e // B_HEADS
    proj_a = _norm_matmul(x2, norm_g[0], a_w_in, tm=1024, tn=2048,
                          epilogues=((2 * e, "gelu"), (e, "silu")))
    x2 = _gmlp_out(proj_a, a_v_g, a_ws, a_bs, a_w_out, x2, tm=256)
    proj_b = _norm_matmul(x2, norm_g[1], b_w_in, tm=1024, tn=2048,
                          epilogues=((e, head_dim ** -0.5 * LOG2E), (2 * e, None), (e, "silu")),
                          col_major=True)
    y = _na_attention(proj_b, rp, bsz=bsz, t=t)
    out = _out_norm(y, b_w_out, x2, final_g, tm=512)
    return out.reshape(bsz, t, d)


def kernel(x_prompt, x_sample, norm_g, final_g, a_w_in, a_v_g, a_ws, a_bs, a_w_out, b_w_in, b_rpb,
           b_w_out):
    assert norm_g.shape[0] == 2 and a_w_in.shape[0] == 1 and b_w_in.shape[0] == 1
    params = (norm_g, final_g, a_w_in[0].astype(BF16), a_v_g[0], a_ws[0].astype(BF16), a_bs[0],
              a_w_out[0].astype(BF16), b_w_in[0].astype(BF16), _pair_table(b_rpb[0]),
              b_w_out[0].astype(BF16))
    return (_trunk(x_prompt, *params), _trunk(x_sample, *params))
```

```python
import functools

import jax
import jax.numpy as jnp
from jax import lax
from jax.experimental import pallas as pl
from jax.experimental.pallas import tpu as pltpu

F32 = jnp.float32
BF16 = jnp.bfloat16

EXPAND = 2
GRID_W = 64
CHUNK = 128
A_GROUPS = 16
B_HEADS = 32
NA_ROWS = 8
NA_COLS = 16
EPS = 1e-6

LANES = 128
VMEM_LIMIT_BYTES = 56 * 1024 * 1024

Q_ROWS = 64
SUB_ROWS = 2
WIN_ROWS = 10
NEG = -1e30
LOG2E = 1.4426950408889634


def _gelu_tanh(x):
    c = 0.7978845608028654
    a = -2.0 * c * 0.044715 * LOG2E
    b = -2.0 * c * LOG2E
    return x / (1.0 + jnp.exp2(x * (a * (x * x) + b)))


def _silu(x):
    return x / (1.0 + jnp.exp2(-LOG2E * x))


def _epilogue(acc, kind):
    if kind == "gelu":
        return _gelu_tanh(acc)
    if kind == "silu":
        return _silu(acc)
    if kind is None:
        return acc
    return acc * kind


def _norm_matmul_kernel(x_ref, g_ref, w_ref, o_ref, h_ref, *, epilogues, col_major):
    j = pl.program_id(1)

    @pl.when(j == 0)
    def _():
        xf = x_ref[...]
        inv = lax.rsqrt(jnp.mean(xf * xf, axis=-1, keepdims=True) + EPS)
        h_ref[...] = ((xf * inv) * g_ref[...]).astype(BF16)

    for lo, hi, kind in epilogues:
        @pl.when((j >= lo) & (j < hi))
        def _(kind=kind):
            acc = jnp.dot(h_ref[...], w_ref[...], preferred_element_type=F32)
            acc = _epilogue(acc, kind)
            if col_major:
                for c in range(o_ref.shape[0]):
                    o_ref[c] = acc[:, c * LANES:(c + 1) * LANES].astype(o_ref.dtype)
            else:
                o_ref[...] = acc.astype(o_ref.dtype)


def _norm_matmul(x, g, w, *, tm, tn, epilogues, col_major=False):
    m, d = x.shape
    n = w.shape[1]
    assert m % tm == 0 and n % tn == 0 and tn % LANES == 0
    tile_epilogues, lo = [], 0
    for n_cols, kind in epilogues:
        assert n_cols % tn == 0
        tile_epilogues.append((lo, lo + n_cols // tn, kind))
        lo += n_cols // tn
    assert lo == n // tn
    if col_major:
        out_shape = jax.ShapeDtypeStruct((n // LANES, m, LANES), BF16)
        out_spec = pl.BlockSpec((tn // LANES, tm, LANES), lambda i, j: (j, i, 0))
    else:
        out_shape = jax.ShapeDtypeStruct((m, n), BF16)
        out_spec = pl.BlockSpec((tm, tn), lambda i, j: (i, j))
    return pl.pallas_call(
        functools.partial(_norm_matmul_kernel, epilogues=tuple(tile_epilogues),
                          col_major=col_major),
        grid=(m // tm, n // tn),
        in_specs=[
            pl.BlockSpec((tm, d), lambda i, j: (i, 0)),
            pl.BlockSpec((1, d), lambda i, j: (0, 0)),
            pl.BlockSpec((d, tn), lambda i, j: (0, j)),
        ],
        out_specs=out_spec,
        out_shape=out_shape,
        scratch_shapes=[pltpu.VMEM((tm, d), BF16)],
        compiler_params=pltpu.CompilerParams(
            dimension_semantics=("arbitrary", "arbitrary"),
            vmem_limit_bytes=VMEM_LIMIT_BYTES),
        name="norm_in_proj",
    )(x, g.reshape(1, d), w)


def _gmlp_out_kernel(u_ref, v_ref, z_ref, vg_ref, ws_ref, bs_ref, wout_ref, x_ref, o_ref, y_ref):
    s = pl.program_id(0)
    tm, e = u_ref.shape
    d = o_ref.shape[1]
    gdim = e // A_GROUPS
    cur = s % 2
    prev = 1 - cur
    n_out_chunks = 8
    out_cols = d // n_out_chunks
    groups_per_chunk = 2 * A_GROUPS // n_out_chunks

    @pl.when(s == 0)
    def _():
        y_ref[1] = jnp.zeros((tm, e), BF16)

    def out_chunk(c):
        cols = slice(c * out_cols, (c + 1) * out_cols)
        acc = jnp.dot(y_ref[prev], wout_ref[:, cols], preferred_element_type=F32)
        o_ref[:, cols] = x_ref[:, cols] + acc

    def sum_squares(g):
        gv = v_ref[:, g * gdim:(g + 1) * gdim].astype(F32)
        return jnp.sum(gv * gv, axis=-1, keepdims=True)

    def gate(g, inv):
        cols = slice(g * gdim, (g + 1) * gdim)
        vn = ((v_ref[:, cols].astype(F32) * inv) * vg_ref[:, cols]).astype(BF16)
        for c in range(tm // CHUNK):
            rows = slice(c * CHUNK, (c + 1) * CHUNK)
            sg = jnp.dot(ws_ref[g], vn[rows], preferred_element_type=F32) + bs_ref[g]
            u = u_ref[rows, cols].astype(F32)
            z = z_ref[rows, cols].astype(F32)
            y_ref[cur, rows, cols] = ((u * sg) * z).astype(BF16)

    ssq = jnp.zeros((tm, 1), F32)
    inv = None
    for unit in range(2 * A_GROUPS):
        if unit % groups_per_chunk == 0:
            out_chunk(unit // groups_per_chunk)
        if unit < A_GROUPS:
            ssq = ssq + sum_squares(unit)
        else:
            if inv is None:
                inv = lax.rsqrt(ssq * (1.0 / e) + EPS)
            gate(unit - A_GROUPS, inv)


def _gmlp_out(proj, v_g, w_s, b_s, w_out, x, *, tm):
    m, d = x.shape
    e = w_out.shape[0]
    assert m % tm == 0 and tm % CHUNK == 0
    n_tiles = m // tm
    gdim = e // A_GROUPS
    cur_tile = lambda s: jnp.minimum(s, n_tiles - 1)
    prev_tile = lambda s: jnp.maximum(s - 1, 0)
    return pl.pallas_call(
        _gmlp_out_kernel,
        grid=(n_tiles + 1,),
        in_specs=[
            pl.BlockSpec((tm, e), lambda s: (cur_tile(s), 0)),
            pl.BlockSpec((tm, e), lambda s: (cur_tile(s), 1)),
            pl.BlockSpec((tm, e), lambda s: (cur_tile(s), 2)),
            pl.BlockSpec((1, e), lambda s: (0, 0)),
            pl.BlockSpec((A_GROUPS, CHUNK, CHUNK), lambda s: (0, 0, 0)),
            pl.BlockSpec((A_GROUPS, CHUNK, gdim), lambda s: (0, 0, 0)),
            pl.BlockSpec((e, d), lambda s: (0, 0), pipeline_mode=pl.Buffered(1)),
            pl.BlockSpec((tm, d), lambda s: (prev_tile(s), 0)),
        ],
        out_specs=pl.BlockSpec((tm, d), lambda s: (prev_tile(s), 0)),
        out_shape=jax.ShapeDtypeStruct((m, d), F32),
        scratch_shapes=[pltpu.VMEM((2, tm, e), BF16)],
        compiler_params=pltpu.CompilerParams(
            dimension_semantics=("arbitrary",),
            vmem_limit_bytes=VMEM_LIMIT_BYTES),
        name="gmlp_gate_out_proj",
    )(proj, proj, proj, v_g.reshape(1, e), w_s,
      jnp.broadcast_to(b_s[:, :, None], (A_GROUPS, CHUNK, gdim)), w_out, x)


def _window_variants():
    rows = 4 * (NA_ROWS + WIN_ROWS)

    def at(r_a):
        r0 = [min(max(r_a + a - NA_ROWS // 2, 0), rows - NA_ROWS) for a in range(SUB_ROWS)]
        ws = min(r0[0], rows - WIN_ROWS)
        return (ws - r_a, tuple(r - ws for r in r0))

    interior = at(rows // 2 // SUB_ROWS * SUB_ROWS)
    tops, bottoms = [], []
    while at(len(tops) * SUB_ROWS) != interior:
        tops.append(at(len(tops) * SUB_ROWS))
    while at(rows - (len(bottoms) + 1) * SUB_ROWS) != interior:
        bottoms.insert(0, at(rows - (len(bottoms) + 1) * SUB_ROWS))
    return tuple(tops) + (interior,) + tuple(bottoms), len(tops), len(bottoms)


_BIAS_VARIANTS, _N_TOP, _N_BOTTOM = _window_variants()
_PAIR_MIN = min(delta for delta, _ in _BIAS_VARIANTS) - (SUB_ROWS - 1)
_PAIR_ROWS = 24
assert max(delta for delta, _ in _BIAS_VARIANTS) + WIN_ROWS - 2 - _PAIR_MIN < _PAIR_ROWS
_NT_DIMS = (((1,), (1,)), ((), ()))


def _build_bias(rp_ref, bias_ref):
    c = lax.broadcasted_iota(jnp.int32, (GRID_W, LANES), 0)
    lane = lax.broadcasted_iota(jnp.int32, (GRID_W, LANES), 1)
    kc = lane & (GRID_W - 1)
    cs = jnp.clip(c - NA_COLS // 2, 0, GRID_W - NA_COLS)
    col_ok = (kc >= cs) & (kc < cs + NA_COLS)
    left = lane < GRID_W
    pair_cache = {}

    def pair(d):
        if d not in pair_cache:
            row = rp_ref[0, d - _PAIR_MIN:d - _PAIR_MIN + 1, :]
            x = jnp.broadcast_to(row, (GRID_W, LANES))
            x = pltpu.roll(x, LANES - (NA_COLS - 1), 1, stride=1, stride_axis=0)
            pair_cache[d] = x * LOG2E
        return pair_cache[d]

    for v, (delta, los) in enumerate(_BIAS_VARIANTS):
        for a in range(SUB_ROWS):
            lo = los[a]
            for p in range(WIN_ROWS // 2):
                ok_l = lo <= 2 * p <= lo + NA_ROWS - 1
                ok_r = lo <= 2 * p + 1 <= lo + NA_ROWS - 1
                if ok_l and ok_r:
                    mask = col_ok
                elif ok_l:
                    mask = col_ok & left
                elif ok_r:
                    mask = col_ok & jnp.logical_not(left)
                else:
                    mask = None
                if mask is None:
                    blk = jnp.full((GRID_W, LANES), NEG, F32)
                else:
                    blk = jnp.where(mask, pair(delta + 2 * p - a), NEG)
                bias_ref[v, a * GRID_W:(a + 1) * GRID_W, p * LANES:(p + 1) * LANES] = blk


def _na_kernel(q_ref, kv_hbm, z_ref, rp_ref, y_ref, bias_ref, kv_ref, kv_sem, *, rows):
    h = pl.program_id(0)
    b = pl.program_id(1)
    rb = pl.program_id(2)
    n_heads = pl.num_programs(0)
    bsz = pl.num_programs(1)
    n_rb = pl.num_programs(2)
    sub_tokens = SUB_ROWS * GRID_W
    win_tokens = WIN_ROWS * GRID_W
    n_sub = Q_ROWS // SUB_ROWS
    slot = (h * bsz + b) % 2

    def kv_copies(head, batch, slot_):
        return [pltpu.make_async_copy(kv_hbm.at[(1 + kv) * n_heads + head, batch],
                                      kv_ref.at[slot_, kv], kv_sem.at[slot_, kv])
                for kv in range(2)]

    @pl.when(rb == 0)
    def _():
        @pl.when((h == 0) & (b == 0))
        def _():
            for cp in kv_copies(h, b, slot):
                cp.start()

        for cp in kv_copies(h, b, slot):
            cp.wait()

        last_batch = b == bsz - 1

        @pl.when(jnp.logical_not(last_batch & (h == n_heads - 1)))
        def _():
            next_h = jnp.where(last_batch, h + 1, h)
            next_b = jnp.where(last_batch, 0, b + 1)
            for cp in kv_copies(next_h, next_b, 1 - slot):
                cp.start()

    @pl.when((b == 0) & (rb == 0))
    def _():
        _build_bias(rp_ref, bias_ref)

    def window_start(i):
        r_a = rb * Q_ROWS + i * SUB_ROWS
        ws = jnp.minimum(jnp.maximum(r_a - NA_ROWS // 2, 0), rows - WIN_ROWS)
        return pl.multiple_of(ws * GRID_W, GRID_W)

    def variant(i):
        v = _N_TOP
        if i < _N_TOP:
            v = jnp.where(rb == 0, i, v)
        if i >= n_sub - _N_BOTTOM:
            v = jnp.where(rb == n_rb - 1, _N_TOP + 1 + i - (n_sub - _N_BOTTOM), v)
        return v

    def scores(i):
        q = q_ref[0, i * sub_tokens:(i + 1) * sub_tokens, :]
        kw = kv_ref[slot, 0, pl.ds(window_start(i), win_tokens), :]
        s = lax.dot_general(q, kw, _NT_DIMS, preferred_element_type=F32)
        return s + bias_ref[variant(i)]

    def softmax(s):
        m = jnp.max(s, axis=-1, keepdims=True)
        p = jnp.exp2(s - m)
        return p.astype(BF16), jnp.sum(p, axis=-1, keepdims=True)

    def finish(i, p, l):
        vw = kv_ref[slot, 1, pl.ds(window_start(i), win_tokens), :]
        o = jnp.dot(p, vw, preferred_element_type=F32) / l
        z = z_ref[0, i * sub_tokens:(i + 1) * sub_tokens, :].astype(F32)
        y_ref[0, i * sub_tokens:(i + 1) * sub_tokens, :] = (o * z).astype(BF16)

    s_next = scores(0)
    pl_cur = None
    for i in range(-1, n_sub):
        s_cur, s_next = s_next, (scores(i + 2) if i + 2 < n_sub else None)
        pl_next = softmax(s_cur) if i + 1 < n_sub else None
        if i >= 0:
            finish(i, *pl_cur)
        pl_cur = pl_next


def _na_attention(proj, rp, *, bsz, t):
    n_blocks, m, hd = proj.shape
    assert n_blocks == 4 * B_HEADS and m == bsz * t and hd == LANES
    rows = t // GRID_W
    assert t % (Q_ROWS * GRID_W) == 0 and rows >= 2 * Q_ROWS and rows >= WIN_ROWS
    assert Q_ROWS // SUB_ROWS >= _N_TOP + _N_BOTTOM and NA_ROWS + SUB_ROWS - 1 <= WIN_ROWS
    n_rb = rows // Q_ROWS
    blk = Q_ROWS * GRID_W
    proj4 = proj.reshape(n_blocks, bsz, t, hd)
    return pl.pallas_call(
        functools.partial(_na_kernel, rows=rows),
        grid=(B_HEADS, bsz, n_rb),
        in_specs=[
            pl.BlockSpec((1, blk, hd), lambda h, b, r: (h, b * n_rb + r, 0)),
            pl.BlockSpec(memory_space=pl.ANY),
            pl.BlockSpec((1, blk, hd), lambda h, b, r: (3 * B_HEADS + h, b * n_rb + r, 0)),
            pl.BlockSpec((1, _PAIR_ROWS, LANES), lambda h, b, r: (h, 0, 0)),
        ],
        out_specs=pl.BlockSpec((1, blk, hd), lambda h, b, r: (h, b * n_rb + r, 0)),
        out_shape=jax.ShapeDtypeStruct((B_HEADS, m, hd), BF16),
        scratch_shapes=[
            pltpu.VMEM((len(_BIAS_VARIANTS), SUB_ROWS * GRID_W, WIN_ROWS * GRID_W), F32),
            pltpu.VMEM((2, 2, t, hd), BF16),
            pltpu.SemaphoreType.DMA((2, 2)),
        ],
        compiler_params=pltpu.CompilerParams(
            dimension_semantics=("arbitrary", "arbitrary", "arbitrary"),
            vmem_limit_bytes=VMEM_LIMIT_BYTES),
        name="na_attention",
    )(proj, proj4, proj, rp)


def _pair_table(rpb):
    h, nr, nc = rpb.shape
    n_off = _PAIR_ROWS + 1
    first = -(NA_ROWS - 1) - _PAIR_MIN
    ext = jnp.zeros((h, n_off, GRID_W), F32)
    ext = ext.at[:, first:first + nr, :nc].set(rpb.astype(F32))
    return jnp.concatenate([ext[:, :-1], ext[:, 1:]], axis=-1)


def _out_norm_kernel(y_ref, w_ref, x_ref, g_ref, o_ref):
    y = jnp.concatenate([y_ref[h] for h in range(y_ref.shape[0])], axis=1)
    acc = jnp.dot(y, w_ref[...], preferred_element_type=F32)
    xn = x_ref[...] + acc
    inv = lax.rsqrt(jnp.mean(xn * xn, axis=-1, keepdims=True) + EPS)
    o_ref[...] = (xn * inv) * g_ref[...]


def _out_norm(y, w_out, x, g, *, tm):
    m, d = x.shape
    e = w_out.shape[0]
    n_heads, _, hd = y.shape
    assert m % tm == 0 and n_heads * hd == e
    return pl.pallas_call(
        _out_norm_kernel,
        grid=(m // tm,),
        in_specs=[
            pl.BlockSpec((n_heads, tm, hd), lambda i: (0, i, 0)),
            pl.BlockSpec((e, d), lambda i: (0, 0), pipeline_mode=pl.Buffered(1)),
            pl.BlockSpec((tm, d), lambda i: (i, 0)),
            pl.BlockSpec((1, d), lambda i: (0, 0)),
        ],
        out_specs=pl.BlockSpec((tm, d), lambda i: (i, 0)),
        out_shape=jax.ShapeDtypeStruct((m, d), F32),
        compiler_params=pltpu.CompilerParams(
            dimension_semantics=("arbitrary",),
            vmem_limit_bytes=VMEM_LIMIT_BYTES),
        name="out_proj_final_norm",
    )(y, w_out, x, g.reshape(1, d))


def _trunk(x, norm_g, final_g, a_w_in, a_v_g, a_ws, a_bs, a_w_out, b_w_in, rp, b_w_out):
    bsz, t, d = x.shape
    x2 = x.reshape(bsz * t, d)
    e = a_w_out.shape[0]
    head_dim = e // B_HEADS
    proj_a = _norm_matmul(x2, norm_g[0], a_w_in, tm=1024, tn=2048,
                          epilogues=((2 * e, "gelu"), (e, "silu")))
    x2 = _gmlp_out(proj_a, a_v_g, a_ws, a_bs, a_w_out, x2, tm=256)
    proj_b = _norm_matmul(x2, norm_g[1], b_w_in, tm=1024, tn=2048,
                          epilogues=((e, head_dim ** -0.5 * LOG2E), (2 * e, None), (e, "silu")),
                          col_major=True)
    y = _na_attention(proj_b, rp, bsz=bsz, t=t)
    out = _out_norm(y, b_w_out, x2, final_g, tm=512)
    return out.reshape(bsz, t, d)


def kernel(x_prompt, x_sample, norm_g, final_g, a_w_in, a_v_g, a_ws, a_bs, a_w_out, b_w_in, b_rpb,
           b_w_out):
    assert norm_g.shape[0] == 2 and a_w_in.shape[0] == 1 and b_w_in.shape[0] == 1
    params = (norm_g, final_g, a_w_in[0].astype(BF16), a_v_g[0], a_ws[0].astype(BF16), a_bs[0],
              a_w_out[0].astype(BF16), b_w_in[0].astype(BF16), _pair_table(b_rpb[0]),
              b_w_out[0].astype(BF16))
    return (_trunk(x_prompt, *params), _trunk(x_sample, *params))
```

```python
import functools

import jax
import jax.numpy as jnp
from jax import lax
from jax.experimental import pallas as pl
from jax.experimental.pallas import tpu as pltpu

F32 = jnp.float32
BF16 = jnp.bfloat16

EXPAND = 2
GRID_W = 64
CHUNK = 128
A_GROUPS = 16
B_HEADS = 32
NA_ROWS = 8
NA_COLS = 16
EPS = 1e-6

LANES = 128
VMEM_LIMIT_BYTES = 56 * 1024 * 1024

Q_ROWS_MAX = 128
SUB_ROWS = 2
WIN_ROWS = 10
NEG = -1e30
LOG2E = 1.4426950408889634


def _gelu_tanh(x):
    c = 0.7978845608028654
    hx = 0.5 * x
    return hx + hx * jnp.tanh(x * ((c * 0.044715) * (x * x) + c))


def _silu(x):
    hx = 0.5 * x
    return hx + hx * jnp.tanh(hx)


def _epilogue(acc, kind):
    if kind == "gelu":
        return _gelu_tanh(acc)
    if kind == "silu":
        return _silu(acc)
    if kind is None:
        return acc
    return acc * kind


def _norm_matmul_kernel(x_ref, g_ref, w_ref, o_ref, h_ref, *, epilogues, col_major):
    j = pl.program_id(1)

    @pl.when(j == 0)
    def _():
        xf = x_ref[...]
        inv = lax.rsqrt(jnp.mean(xf * xf, axis=-1, keepdims=True) + EPS)
        h_ref[...] = ((xf * inv) * g_ref[...]).astype(BF16)

    for lo, hi, kind in epilogues:
        @pl.when((j >= lo) & (j < hi))
        def _(kind=kind):
            acc = jnp.dot(h_ref[...], w_ref[...], preferred_element_type=F32)
            acc = _epilogue(acc, kind)
            if col_major:
                for c in range(o_ref.shape[0]):
                    o_ref[c] = acc[:, c * LANES:(c + 1) * LANES].astype(o_ref.dtype)
            else:
                o_ref[...] = acc.astype(o_ref.dtype)


def _norm_matmul(x, g, w, *, tm, tn, epilogues, col_major=False):
    m, d = x.shape
    n = w.shape[1]
    assert m % tm == 0 and n % tn == 0 and tn % LANES == 0
    tile_epilogues, lo = [], 0
    for n_cols, kind in epilogues:
        assert n_cols % tn == 0
        tile_epilogues.append((lo, lo + n_cols // tn, kind))
        lo += n_cols // tn
    assert lo == n // tn
    if col_major:
        out_shape = jax.ShapeDtypeStruct((n // LANES, m, LANES), BF16)
        out_spec = pl.BlockSpec((tn // LANES, tm, LANES), lambda i, j: (j, i, 0))
    else:
        out_shape = jax.ShapeDtypeStruct((m, n), BF16)
        out_spec = pl.BlockSpec((tm, tn), lambda i, j: (i, j))
    return pl.pallas_call(
        functools.partial(_norm_matmul_kernel, epilogues=tuple(tile_epilogues),
                          col_major=col_major),
        grid=(m // tm, n // tn),
        in_specs=[
            pl.BlockSpec((tm, d), lambda i, j: (i, 0)),
            pl.BlockSpec((1, d), lambda i, j: (0, 0)),
            pl.BlockSpec((d, tn), lambda i, j: (0, j)),
        ],
        out_specs=out_spec,
        out_shape=out_shape,
        scratch_shapes=[pltpu.VMEM((tm, d), BF16)],
        compiler_params=pltpu.CompilerParams(
            dimension_semantics=("arbitrary", "arbitrary"),
            vmem_limit_bytes=VMEM_LIMIT_BYTES),
        name="norm_in_proj",
    )(x, g.reshape(1, d), w)


def _gmlp_out_kernel(u_ref, v_ref, z_ref, vg_ref, ws_ref, bs_ref, wout_ref, x_ref, o_ref, y_ref):
    s = pl.program_id(0)
    tm, e = u_ref.shape
    gdim = e // A_GROUPS
    cur = s % 2
    prev = 1 - cur

    @pl.when(s == 0)
    def _():
        y_ref[1] = jnp.zeros((tm, e), BF16)

    acc = jnp.dot(y_ref[prev], wout_ref[...], preferred_element_type=F32)
    o_ref[...] = x_ref[...] + acc

    ssq = jnp.zeros((tm, 1), F32)
    for g in range(A_GROUPS):
        gv = v_ref[:, g * gdim:(g + 1) * gdim].astype(F32)
        ssq = ssq + jnp.sum(gv * gv, axis=-1, keepdims=True)
    inv = lax.rsqrt(ssq * (1.0 / e) + EPS)

    for g in range(A_GROUPS):
        cols = slice(g * gdim, (g + 1) * gdim)
        vn = ((v_ref[:, cols].astype(F32) * inv) * vg_ref[:, cols]).astype(BF16)
        for c in range(tm // CHUNK):
            rows = slice(c * CHUNK, (c + 1) * CHUNK)
            sg = jnp.dot(ws_ref[g], vn[rows], preferred_element_type=F32) + bs_ref[g]
            u = u_ref[rows, cols].astype(F32)
            z = z_ref[rows, cols].astype(F32)
            y_ref[cur, rows, cols] = ((u * sg) * z).astype(BF16)


def _gmlp_out(proj, v_g, w_s, b_s, w_out, x, *, tm):
    m, d = x.shape
    e = w_out.shape[0]
    assert m % tm == 0 and tm % CHUNK == 0
    n_tiles = m // tm
    gdim = e // A_GROUPS
    cur_tile = lambda s: jnp.minimum(s, n_tiles - 1)
    prev_tile = lambda s: jnp.maximum(s - 1, 0)
    return pl.pallas_call(
        _gmlp_out_kernel,
        grid=(n_tiles + 1,),
        in_specs=[
            pl.BlockSpec((tm, e), lambda s: (cur_tile(s), 0)),
            pl.BlockSpec((tm, e), lambda s: (cur_tile(s), 1)),
            pl.BlockSpec((tm, e), lambda s: (cur_tile(s), 2)),
            pl.BlockSpec((1, e), lambda s: (0, 0)),
            pl.BlockSpec((A_GROUPS, CHUNK, CHUNK), lambda s: (0, 0, 0)),
            pl.BlockSpec((A_GROUPS, CHUNK, gdim), lambda s: (0, 0, 0)),
            pl.BlockSpec((e, d), lambda s: (0, 0), pipeline_mode=pl.Buffered(1)),
            pl.BlockSpec((tm, d), lambda s: (prev_tile(s), 0)),
        ],
        out_specs=pl.BlockSpec((tm, d), lambda s: (prev_tile(s), 0)),
        out_shape=jax.ShapeDtypeStruct((m, d), F32),
        scratch_shapes=[pltpu.VMEM((2, tm, e), BF16)],
        compiler_params=pltpu.CompilerParams(
            dimension_semantics=("arbitrary",),
            vmem_limit_bytes=VMEM_LIMIT_BYTES),
        name="gmlp_gate_out_proj",
    )(proj, proj, proj, v_g.reshape(1, e), w_s,
      jnp.broadcast_to(b_s[:, :, None], (A_GROUPS, CHUNK, gdim)), w_out, x)


def _window_variants():
    rows = 4 * (NA_ROWS + WIN_ROWS)

    def at(r_a):
        r0 = [min(max(r_a + a - NA_ROWS // 2, 0), rows - NA_ROWS) for a in range(SUB_ROWS)]
        ws = min(r0[0], rows - WIN_ROWS)
        return (ws - r_a, tuple(r - ws for r in r0))

    interior = at(rows // 2 // SUB_ROWS * SUB_ROWS)
    tops, bottoms = [], []
    while at(len(tops) * SUB_ROWS) != interior:
        tops.append(at(len(tops) * SUB_ROWS))
    while at(rows - (len(bottoms) + 1) * SUB_ROWS) != interior:
        bottoms.insert(0, at(rows - (len(bottoms) + 1) * SUB_ROWS))
    return tuple(tops) + (interior,) + tuple(bottoms), len(tops), len(bottoms)


_BIAS_VARIANTS, _N_TOP, _N_BOTTOM = _window_variants()
_PAIR_MIN = min(delta for delta, _ in _BIAS_VARIANTS) - (SUB_ROWS - 1)
_PAIR_ROWS = 24
assert max(delta for delta, _ in _BIAS_VARIANTS) + WIN_ROWS - 2 - _PAIR_MIN < _PAIR_ROWS
_NT_DIMS = (((1,), (1,)), ((), ()))


def _build_bias(rp_ref, bias_ref):
    c = lax.broadcasted_iota(jnp.int32, (GRID_W, LANES), 0)
    lane = lax.broadcasted_iota(jnp.int32, (GRID_W, LANES), 1)
    kc = lane & (GRID_W - 1)
    cs = jnp.clip(c - NA_COLS // 2, 0, GRID_W - NA_COLS)
    col_ok = (kc >= cs) & (kc < cs + NA_COLS)
    left = lane < GRID_W
    pair_cache = {}

    def pair(d):
        if d not in pair_cache:
            row = rp_ref[0, d - _PAIR_MIN:d - _PAIR_MIN + 1, :]
            x = jnp.broadcast_to(row, (GRID_W, LANES))
            x = pltpu.roll(x, LANES - (NA_COLS - 1), 1, stride=1, stride_axis=0)
            pair_cache[d] = x * LOG2E
        return pair_cache[d]

    for v, (delta, los) in enumerate(_BIAS_VARIANTS):
        for a in range(SUB_ROWS):
            lo = los[a]
            for p in range(WIN_ROWS // 2):
                ok_l = lo <= 2 * p <= lo + NA_ROWS - 1
                ok_r = lo <= 2 * p + 1 <= lo + NA_ROWS - 1
                if ok_l and ok_r:
                    mask = col_ok
                elif ok_l:
                    mask = col_ok & left
                elif ok_r:
                    mask = col_ok & jnp.logical_not(left)
                else:
                    mask = None
                if mask is None:
                    blk = jnp.full((GRID_W, LANES), NEG, F32)
                else:
                    blk = jnp.where(mask, pair(delta + 2 * p - a), NEG)
                bias_ref[v, a * GRID_W:(a + 1) * GRID_W, p * LANES:(p + 1) * LANES] = blk


def _na_kernel(q_ref, kv_hbm, z_ref, rp_ref, y_ref, bias_ref, kv_ref, kv_sem, *, rows, q_rows):
    h = pl.program_id(0)
    b = pl.program_id(1)
    rb = pl.program_id(2)
    n_heads = pl.num_programs(0)
    bsz = pl.num_programs(1)
    n_rb = pl.num_programs(2)
    sub_tokens = SUB_ROWS * GRID_W
    win_tokens = WIN_ROWS * GRID_W
    n_sub = q_rows // SUB_ROWS
    slot = (h * bsz + b) % 2

    def kv_copies(head, batch, slot_):
        return [pltpu.make_async_copy(kv_hbm.at[(1 + kv) * n_heads + head, batch],
                                      kv_ref.at[slot_, kv], kv_sem.at[slot_, kv])
                for kv in range(2)]

    @pl.when(rb == 0)
    def _():
        @pl.when((h == 0) & (b == 0))
        def _():
            for cp in kv_copies(h, b, slot):
                cp.start()

        for cp in kv_copies(h, b, slot):
            cp.wait()

        last_batch = b == bsz - 1

        @pl.when(jnp.logical_not(last_batch & (h == n_heads - 1)))
        def _():
            next_h = jnp.where(last_batch, h + 1, h)
            next_b = jnp.where(last_batch, 0, b + 1)
            for cp in kv_copies(next_h, next_b, 1 - slot):
                cp.start()

    @pl.when((b == 0) & (rb == 0))
    def _():
        _build_bias(rp_ref, bias_ref)

    def window_start(i):
        r_a = rb * q_rows + i * SUB_ROWS
        ws = jnp.minimum(jnp.maximum(r_a - NA_ROWS // 2, 0), rows - WIN_ROWS)
        return pl.multiple_of(ws * GRID_W, GRID_W)

    def variant(i):
        v = _N_TOP
        if i < _N_TOP:
            v = jnp.where(rb == 0, i, v)
        if i >= n_sub - _N_BOTTOM:
            v = jnp.where(rb == n_rb - 1, _N_TOP + 1 + i - (n_sub - _N_BOTTOM), v)
        return v

    def scores(i):
        q = q_ref[0, i * sub_tokens:(i + 1) * sub_tokens, :]
        kw = kv_ref[slot, 0, pl.ds(window_start(i), win_tokens), :]
        s = lax.dot_general(q, kw, _NT_DIMS, preferred_element_type=F32)
        return s + bias_ref[variant(i)]

    def softmax(s):
        m = jnp.max(s, axis=-1, keepdims=True)
        p = jnp.exp2(s - m)
        return p.astype(BF16), jnp.sum(p, axis=-1, keepdims=True)

    def finish(i, p, l):
        vw = kv_ref[slot, 1, pl.ds(window_start(i), win_tokens), :]
        o = jnp.dot(p, vw, preferred_element_type=F32) / l
        z = z_ref[0, i * sub_tokens:(i + 1) * sub_tokens, :].astype(F32)
        y_ref[0, i * sub_tokens:(i + 1) * sub_tokens, :] = (o * z).astype(BF16)

    s_next = scores(0)
    pl_cur = None
    for i in range(-1, n_sub):
        s_cur, s_next = s_next, (scores(i + 2) if i + 2 < n_sub else None)
        pl_next = softmax(s_cur) if i + 1 < n_sub else None
        if i >= 0:
            finish(i, *pl_cur)
        pl_cur = pl_next


def _na_attention(proj, rp, *, bsz, t):
    n_blocks, m, hd = proj.shape
    assert n_blocks == 4 * B_HEADS and m == bsz * t and hd == LANES
    rows = t // GRID_W
    q_rows = min(Q_ROWS_MAX, rows // 2)
    assert rows % q_rows == 0 and q_rows % SUB_ROWS == 0 and rows >= WIN_ROWS
    assert q_rows // SUB_ROWS >= _N_TOP + _N_BOTTOM and NA_ROWS + SUB_ROWS - 1 <= WIN_ROWS
    n_rb = rows // q_rows
    blk = q_rows * GRID_W
    proj4 = proj.reshape(n_blocks, bsz, t, hd)
    return pl.pallas_call(
        functools.partial(_na_kernel, rows=rows, q_rows=q_rows),
        grid=(B_HEADS, bsz, n_rb),
        in_specs=[
            pl.BlockSpec((1, blk, hd), lambda h, b, r: (h, b * n_rb + r, 0)),
            pl.BlockSpec(memory_space=pl.ANY),
            pl.BlockSpec((1, blk, hd), lambda h, b, r: (3 * B_HEADS + h, b * n_rb + r, 0)),
            pl.BlockSpec((1, _PAIR_ROWS, LANES), lambda h, b, r: (h, 0, 0)),
        ],
        out_specs=pl.BlockSpec((1, blk, hd), lambda h, b, r: (h, b * n_rb + r, 0)),
        out_shape=jax.ShapeDtypeStruct((B_HEADS, m, hd), BF16),
        scratch_shapes=[
            pltpu.VMEM((len(_BIAS_VARIANTS), SUB_ROWS * GRID_W, WIN_ROWS * GRID_W), F32),
            pltpu.VMEM((2, 2, t, hd), BF16),
            pltpu.SemaphoreType.DMA((2, 2)),
        ],
        compiler_params=pltpu.CompilerParams(
            dimension_semantics=("arbitrary", "arbitrary", "arbitrary"),
            vmem_limit_bytes=VMEM_LIMIT_BYTES),
        name="na_attention",
    )(proj, proj4, proj, rp)


def _pair_table(rpb):
    h, nr, nc = rpb.shape
    n_off = _PAIR_ROWS + 1
    first = -(NA_ROWS - 1) - _PAIR_MIN
    ext = jnp.zeros((h, n_off, GRID_W), F32)
    ext = ext.at[:, first:first + nr, :nc].set(rpb.astype(F32))
    return jnp.concatenate([ext[:, :-1], ext[:, 1:]], axis=-1)


def _out_norm_kernel(y_ref, w_ref, x_ref, g_ref, o_ref):
    y = jnp.concatenate([y_ref[h] for h in range(y_ref.shape[0])], axis=1)
    acc = jnp.dot(y, w_ref[...], preferred_element_type=F32)
    xn = x_ref[...] + acc
    inv = lax.rsqrt(jnp.mean(xn * xn, axis=-1, keepdims=True) + EPS)
    o_ref[...] = (xn * inv) * g_ref[...]


def _out_norm(y, w_out, x, g, *, tm):
    m, d = x.shape
    e = w_out.shape[0]
    n_heads, _, hd = y.shape
    assert m % tm == 0 and n_heads * hd == e
    return pl.pallas_call(
        _out_norm_kernel,
        grid=(m // tm,),
        in_specs=[
            pl.BlockSpec((n_heads, tm, hd), lambda i: (0, i, 0)),
            pl.BlockSpec((e, d), lambda i: (0, 0), pipeline_mode=pl.Buffered(1)),
            pl.BlockSpec((tm, d), lambda i: (i, 0)),
            pl.BlockSpec((1, d), lambda i: (0, 0)),
        ],
        out_specs=pl.BlockSpec((tm, d), lambda i: (i, 0)),
        out_shape=jax.ShapeDtypeStruct((m, d), F32),
        compiler_params=pltpu.CompilerParams(
            dimension_semantics=("arbitrary",),
            vmem_limit_bytes=VMEM_LIMIT_BYTES),
        name="out_proj_final_norm",
    )(y, w_out, x, g.reshape(1, d))


def _trunk(x, norm_g, final_g, a_w_in, a_v_g, a_ws, a_bs, a_w_out, b_w_in, rp, b_w_out):
    bsz, t, d = x.shape
    x2 = x.reshape(bsz * t, d)
    e = a_w_out.shape[0]
    head_dim = e // B_HEADS
    proj_a = _norm_matmul(x2, norm_g[0], a_w_in, tm=1024, tn=2048,
                          epilogues=((2 * e, "gelu"), (e, "silu")))
    x2 = _gmlp_out(proj_a, a_v_g, a_ws, a_bs, a_w_out, x2, tm=256)
    proj_b = _norm_matmul(x2, norm_g[1], b_w_in, tm=1024, tn=2048,
                          epilogues=((e, head_dim ** -0.5 * LOG2E), (2 * e, None), (e, "silu")),
                          col_major=True)
    y = _na_attention(proj_b, rp, bsz=bsz, t=t)
    out = _out_norm(y, b_w_out, x2, final_g, tm=512)
    return out.reshape(bsz, t, d)


def kernel(x_prompt, x_sample, norm_g, final_g, a_w_in, a_v_g, a_ws, a_bs, a_w_out, b_w_in, b_rpb,
           b_w_out):
    assert norm_g.shape[0] == 2 and a_w_in.shape[0] == 1 and b_w_in.shape[0] == 1
    params = (norm_g, final_g, a_w_in[0].astype(BF16), a_v_g[0], a_ws[0].astype(BF16), a_bs[0],
              a_w_out[0].astype(BF16), b_w_in[0].astype(BF16), _pair_table(b_rpb[0]),
              b_w_out[0].astype(BF16))
    return (_trunk(x_prompt, *params), _trunk(x_sample, *params))
```

```python
import functools

import jax
import jax.numpy as jnp
from jax import lax
from jax.experimental import pallas as pl
from jax.experimental.pallas import tpu as pltpu

F32 = jnp.float32
BF16 = jnp.bfloat16

GRID_W = 64
CHUNK = 128
A_GROUPS = 16
B_HEADS = 32
NA_ROWS = 8
NA_COLS = 16
EPS = 1e-6

LANES = 128
VMEM_LIMIT_BYTES = 56 * 1024 * 1024

IN_PROJ_TM = 1024
IN_PROJ_TN = 2048
GMLP_TM = 256
OUT_PROJ_TM = 512

Q_ROWS_MAX = 128
SUB_ROWS = 2
WIN_ROWS = 10
NEG = -1e30
LOG2E = 1.4426950408889634


def _gelu_tanh(x):
    c = 0.7978845608028654
    hx = 0.5 * x
    return hx + hx * jnp.tanh(x * ((c * 0.044715) * (x * x) + c))


def _silu(x):
    hx = 0.5 * x
    return hx + hx * jnp.tanh(hx)


def _epilogue(acc, kind):
    if kind == "gelu":
        return _gelu_tanh(acc)
    if kind == "silu":
        return _silu(acc)
    if kind is None:
        return acc
    return acc * kind


def _norm_matmul_kernel(x_ref, g_ref, w_ref, o_ref, h_ref, *, epilogues, col_major):
    j = pl.program_id(1)

    @pl.when(j == 0)
    def _():
        xf = x_ref[...]
        inv = lax.rsqrt(jnp.mean(xf * xf, axis=-1, keepdims=True) + EPS)
        h_ref[...] = ((xf * inv) * g_ref[...]).astype(BF16)

    for lo, hi, kind in epilogues:
        @pl.when((j >= lo) & (j < hi))
        def _(kind=kind):
            acc = jnp.dot(h_ref[...], w_ref[...], preferred_element_type=F32)
            acc = _epilogue(acc, kind)
            if col_major:
                for c in range(o_ref.shape[0]):
                    o_ref[c] = acc[:, c * LANES:(c + 1) * LANES].astype(o_ref.dtype)
            else:
                o_ref[...] = acc.astype(o_ref.dtype)


def _norm_matmul(x, g, w, *, tm, tn, epilogues, col_major=False):
    m, d = x.shape
    n = w.shape[1]
    assert m % tm == 0 and n % tn == 0 and tn % LANES == 0
    tile_epilogues, lo = [], 0
    for n_cols, kind in epilogues:
        assert n_cols % tn == 0
        tile_epilogues.append((lo, lo + n_cols // tn, kind))
        lo += n_cols // tn
    assert lo == n // tn
    if col_major:
        out_shape = jax.ShapeDtypeStruct((n // LANES, m, LANES), BF16)
        out_spec = pl.BlockSpec((tn // LANES, tm, LANES), lambda i, j: (j, i, 0))
    else:
        out_shape = jax.ShapeDtypeStruct((m, n), BF16)
        out_spec = pl.BlockSpec((tm, tn), lambda i, j: (i, j))
    return pl.pallas_call(
        functools.partial(_norm_matmul_kernel, epilogues=tuple(tile_epilogues),
                          col_major=col_major),
        grid=(m // tm, n // tn),
        in_specs=[
            pl.BlockSpec((tm, d), lambda i, j: (i, 0)),
            pl.BlockSpec((1, d), lambda i, j: (0, 0)),
            pl.BlockSpec((d, tn), lambda i, j: (0, j)),
        ],
        out_specs=out_spec,
        out_shape=out_shape,
        scratch_shapes=[pltpu.VMEM((tm, d), BF16)],
        compiler_params=pltpu.CompilerParams(
            dimension_semantics=("arbitrary", "arbitrary"),
            vmem_limit_bytes=VMEM_LIMIT_BYTES),
        name="norm_in_proj",
    )(x, g.reshape(1, d), w)


def _gmlp_out_kernel(u_ref, v_ref, z_ref, vg_ref, ws_ref, bs_ref, wout_ref, x_ref, o_ref, y_ref):
    s = pl.program_id(0)
    tm, e = u_ref.shape
    gdim = e // A_GROUPS
    cur = s % 2
    prev = 1 - cur

    @pl.when(s == 0)
    def _():
        y_ref[1] = jnp.zeros((tm, e), BF16)

    acc = jnp.dot(y_ref[prev], wout_ref[...], preferred_element_type=F32)
    o_ref[...] = x_ref[...] + acc

    ssq = jnp.zeros((tm, 1), F32)
    for g in range(A_GROUPS):
        gv = v_ref[:, g * gdim:(g + 1) * gdim].astype(F32)
        ssq = ssq + jnp.sum(gv * gv, axis=-1, keepdims=True)
    inv = lax.rsqrt(ssq * (1.0 / e) + EPS)

    for g in range(A_GROUPS):
        cols = slice(g * gdim, (g + 1) * gdim)
        vn = ((v_ref[:, cols].astype(F32) * inv) * vg_ref[:, cols]).astype(BF16)
        for c in range(tm // CHUNK):
            rows = slice(c * CHUNK, (c + 1) * CHUNK)
            sg = jnp.dot(ws_ref[g], vn[rows], preferred_element_type=F32) + bs_ref[g]
            u = u_ref[rows, cols].astype(F32)
            z = z_ref[rows, cols].astype(F32)
            y_ref[cur, rows, cols] = ((u * sg) * z).astype(BF16)


def _gmlp_out(proj, v_g, w_s, b_s, w_out, x, *, tm):
    m, d = x.shape
    e = w_out.shape[0]
    assert m % tm == 0 and tm % CHUNK == 0
    n_tiles = m // tm
    gdim = e // A_GROUPS
    cur_tile = lambda s: jnp.minimum(s, n_tiles - 1)
    prev_tile = lambda s: jnp.maximum(s - 1, 0)
    return pl.pallas_call(
        _gmlp_out_kernel,
        grid=(n_tiles + 1,),
        in_specs=[
            pl.BlockSpec((tm, e), lambda s: (cur_tile(s), 0)),
            pl.BlockSpec((tm, e), lambda s: (cur_tile(s), 1)),
            pl.BlockSpec((tm, e), lambda s: (cur_tile(s), 2)),
            pl.BlockSpec((1, e), lambda s: (0, 0)),
            pl.BlockSpec((A_GROUPS, CHUNK, CHUNK), lambda s: (0, 0, 0)),
            pl.BlockSpec((A_GROUPS, CHUNK, gdim), lambda s: (0, 0, 0)),
            pl.BlockSpec((e, d), lambda s: (0, 0), pipeline_mode=pl.Buffered(1)),
            pl.BlockSpec((tm, d), lambda s: (prev_tile(s), 0)),
        ],
        out_specs=pl.BlockSpec((tm, d), lambda s: (prev_tile(s), 0)),
        out_shape=jax.ShapeDtypeStruct((m, d), F32),
        scratch_shapes=[pltpu.VMEM((2, tm, e), BF16)],
        compiler_params=pltpu.CompilerParams(
            dimension_semantics=("arbitrary",),
            vmem_limit_bytes=VMEM_LIMIT_BYTES),
        name="gmlp_gate_out_proj",
    )(proj, proj, proj, v_g.reshape(1, e), w_s,
      jnp.broadcast_to(b_s[:, :, None], (A_GROUPS, CHUNK, gdim)), w_out, x)


def _window_variants():
    rows = 4 * (NA_ROWS + WIN_ROWS)

    def at(r_a):
        r0 = [min(max(r_a + a - NA_ROWS // 2, 0), rows - NA_ROWS) for a in range(SUB_ROWS)]
        ws = min(r0[0], rows - WIN_ROWS)
        return (ws - r_a, tuple(r - ws for r in r0))

    interior = at(rows // 2 // SUB_ROWS * SUB_ROWS)
    tops, bottoms = [], []
    while at(len(tops) * SUB_ROWS) != interior:
        tops.append(at(len(tops) * SUB_ROWS))
    while at(rows - (len(bottoms) + 1) * SUB_ROWS) != interior:
        bottoms.insert(0, at(rows - (len(bottoms) + 1) * SUB_ROWS))
    return tuple(tops) + (interior,) + tuple(bottoms), len(tops), len(bottoms)


_BIAS_VARIANTS, _N_TOP, _N_BOTTOM = _window_variants()
_PAIR_MIN = min(delta for delta, _ in _BIAS_VARIANTS) - (SUB_ROWS - 1)
_PAIR_ROWS = 24
assert max(delta for delta, _ in _BIAS_VARIANTS) + WIN_ROWS - 2 - _PAIR_MIN < _PAIR_ROWS
_NT_DIMS = (((1,), (1,)), ((), ()))


def _build_bias(rp_ref, bias_ref):
    c = lax.broadcasted_iota(jnp.int32, (GRID_W, LANES), 0)
    lane = lax.broadcasted_iota(jnp.int32, (GRID_W, LANES), 1)
    kc = lane & (GRID_W - 1)
    cs = jnp.clip(c - NA_COLS // 2, 0, GRID_W - NA_COLS)
    col_ok = (kc >= cs) & (kc < cs + NA_COLS)
    left = lane < GRID_W
    pair_cache = {}

    def pair(d):
        if d not in pair_cache:
            row = rp_ref[0, d - _PAIR_MIN:d - _PAIR_MIN + 1, :]
            x = jnp.broadcast_to(row, (GRID_W, LANES))
            x = pltpu.roll(x, LANES - (NA_COLS - 1), 1, stride=1, stride_axis=0)
            pair_cache[d] = x * LOG2E
        return pair_cache[d]

    for v, (delta, los) in enumerate(_BIAS_VARIANTS):
        for a in range(SUB_ROWS):
            lo = los[a]
            for p in range(WIN_ROWS // 2):
                ok_l = lo <= 2 * p <= lo + NA_ROWS - 1
                ok_r = lo <= 2 * p + 1 <= lo + NA_ROWS - 1
                if ok_l and ok_r:
                    mask = col_ok
                elif ok_l:
                    mask = col_ok & left
                elif ok_r:
                    mask = col_ok & jnp.logical_not(left)
                else:
                    mask = None
                if mask is None:
                    blk = jnp.full((GRID_W, LANES), NEG, F32)
                else:
                    blk = jnp.where(mask, pair(delta + 2 * p - a), NEG)
                bias_ref[v, a * GRID_W:(a + 1) * GRID_W, p * LANES:(p + 1) * LANES] = blk


def _na_kernel(q_ref, kv_hbm, z_ref, rp_ref, y_ref, bias_ref, kv_ref, kv_sem, *, rows, q_rows):
    h = pl.program_id(0)
    b = pl.program_id(1)
    rb = pl.program_id(2)
    n_heads = pl.num_programs(0)
    bsz = pl.num_programs(1)
    n_rb = pl.num_programs(2)
    sub_tokens = SUB_ROWS * GRID_W
    win_tokens = WIN_ROWS * GRID_W
    n_sub = q_rows // SUB_ROWS
    slot = (h * bsz + b) % 2

    def kv_copies(head, batch, slot_):
        return [pltpu.make_async_copy(kv_hbm.at[(1 + kv) * n_heads + head, batch],
                                      kv_ref.at[slot_, kv], kv_sem.at[slot_, kv])
                for kv in range(2)]

    @pl.when(rb == 0)
    def _():
        @pl.when((h == 0) & (b == 0))
        def _():
            for cp in kv_copies(h, b, slot):
                cp.start()

        for cp in kv_copies(h, b, slot):
            cp.wait()

        last_batch = b == bsz - 1

        @pl.when(jnp.logical_not(last_batch & (h == n_heads - 1)))
        def _():
            next_h = jnp.where(last_batch, h + 1, h)
            next_b = jnp.where(last_batch, 0, b + 1)
            for cp in kv_copies(next_h, next_b, 1 - slot):
                cp.start()

    @pl.when((b == 0) & (rb == 0))
    def _():
        _build_bias(rp_ref, bias_ref)

    def window_start(i):
        r_a = rb * q_rows + i * SUB_ROWS
        ws = jnp.minimum(jnp.maximum(r_a - NA_ROWS // 2, 0), rows - WIN_ROWS)
        return pl.multiple_of(ws * GRID_W, GRID_W)

    def variant(i):
        v = _N_TOP
        if i < _N_TOP:
            v = jnp.where(rb == 0, i, v)
        if i >= n_sub - _N_BOTTOM:
            v = jnp.where(rb == n_rb - 1, _N_TOP + 1 + i - (n_sub - _N_BOTTOM), v)
        return v

    def scores(i):
        q = q_ref[0, i * sub_tokens:(i + 1) * sub_tokens, :]
        kw = kv_ref[slot, 0, pl.ds(window_start(i), win_tokens), :]
        s = lax.dot_general(q, kw, _NT_DIMS, preferred_element_type=F32)
        return s + bias_ref[variant(i)]

    def softmax(s):
        m = jnp.max(s, axis=-1, keepdims=True)
        p = jnp.exp2(s - m)
        return p.astype(BF16), jnp.sum(p, axis=-1, keepdims=True)

    def finish(i, p, l):
        vw = kv_ref[slot, 1, pl.ds(window_start(i), win_tokens), :]
        o = jnp.dot(p, vw, preferred_element_type=F32) / l
        z = z_ref[0, i * sub_tokens:(i + 1) * sub_tokens, :].astype(F32)
        y_ref[0, i * sub_tokens:(i + 1) * sub_tokens, :] = (o * z).astype(BF16)

    s_next = scores(0)
    pl_cur = None
    for i in range(-1, n_sub):
        s_cur, s_next = s_next, (scores(i + 2) if i + 2 < n_sub else None)
        pl_next = softmax(s_cur) if i + 1 < n_sub else None
        if i >= 0:
            finish(i, *pl_cur)
        pl_cur = pl_next


def _na_attention(proj, rp, *, bsz, t):
    n_blocks, m, hd = proj.shape
    assert n_blocks == 4 * B_HEADS and m == bsz * t and hd == LANES
    rows = t // GRID_W
    q_rows = min(Q_ROWS_MAX, rows // 2)
    assert rows % q_rows == 0 and q_rows % SUB_ROWS == 0 and rows >= WIN_ROWS
    assert q_rows // SUB_ROWS >= _N_TOP + _N_BOTTOM and NA_ROWS + SUB_ROWS - 1 <= WIN_ROWS
    n_rb = rows // q_rows
    blk = q_rows * GRID_W
    proj4 = proj.reshape(n_blocks, bsz, t, hd)
    return pl.pallas_call(
        functools.partial(_na_kernel, rows=rows, q_rows=q_rows),
        grid=(B_HEADS, bsz, n_rb),
        in_specs=[
            pl.BlockSpec((1, blk, hd), lambda h, b, r: (h, b * n_rb + r, 0)),
            pl.BlockSpec(memory_space=pl.ANY),
            pl.BlockSpec((1, blk, hd), lambda h, b, r: (3 * B_HEADS + h, b * n_rb + r, 0)),
            pl.BlockSpec((1, _PAIR_ROWS, LANES), lambda h, b, r: (h, 0, 0)),
        ],
        out_specs=pl.BlockSpec((1, blk, hd), lambda h, b, r: (h, b * n_rb + r, 0)),
        out_shape=jax.ShapeDtypeStruct((B_HEADS, m, hd), BF16),
        scratch_shapes=[
            pltpu.VMEM((len(_BIAS_VARIANTS), SUB_ROWS * GRID_W, WIN_ROWS * GRID_W), F32),
            pltpu.VMEM((2, 2, t, hd), BF16),
            pltpu.SemaphoreType.DMA((2, 2)),
        ],
        compiler_params=pltpu.CompilerParams(
            dimension_semantics=("arbitrary", "arbitrary", "arbitrary"),
            vmem_limit_bytes=VMEM_LIMIT_BYTES),
        name="na_attention",
    )(proj, proj4, proj, rp)


def _pair_table(rpb):
    h, nr, nc = rpb.shape
    n_off = _PAIR_ROWS + 1
    first = -(NA_ROWS - 1) - _PAIR_MIN
    ext = jnp.zeros((h, n_off, GRID_W), F32)
    ext = ext.at[:, first:first + nr, :nc].set(rpb.astype(F32))
    return jnp.concatenate([ext[:, :-1], ext[:, 1:]], axis=-1)


def _out_norm_kernel(y_ref, w_ref, x_ref, g_ref, o_ref):
    y = jnp.concatenate([y_ref[h] for h in range(y_ref.shape[0])], axis=1)
    acc = jnp.dot(y, w_ref[...], preferred_element_type=F32)
    xn = x_ref[...] + acc
    inv = lax.rsqrt(jnp.mean(xn * xn, axis=-1, keepdims=True) + EPS)
    o_ref[...] = (xn * inv) * g_ref[...]


def _out_norm(y, w_out, x, g, *, tm):
    m, d = x.shape
    e = w_out.shape[0]
    n_heads, _, hd = y.shape
    assert m % tm == 0 and n_heads * hd == e
    return pl.pallas_call(
        _out_norm_kernel,
        grid=(m // tm,),
        in_specs=[
            pl.BlockSpec((n_heads, tm, hd), lambda i: (0, i, 0)),
            pl.BlockSpec((e, d), lambda i: (0, 0), pipeline_mode=pl.Buffered(1)),
            pl.BlockSpec((tm, d), lambda i: (i, 0)),
            pl.BlockSpec((1, d), lambda i: (0, 0)),
        ],
        out_specs=pl.BlockSpec((tm, d), lambda i: (i, 0)),
        out_shape=jax.ShapeDtypeStruct((m, d), F32),
        compiler_params=pltpu.CompilerParams(
            dimension_semantics=("arbitrary",),
            vmem_limit_bytes=VMEM_LIMIT_BYTES),
        name="out_proj_final_norm",
    )(y, w_out, x, g.reshape(1, d))


def _trunk(x, norm_g, final_g, a_w_in, a_v_g, a_ws, a_bs, a_w_out, b_w_in, rp, b_w_out):
    bsz, t, d = x.shape
    x2 = x.reshape(bsz * t, d)
    e = a_w_out.shape[0]
    head_dim = e // B_HEADS
    proj_a = _norm_matmul(x2, norm_g[0], a_w_in, tm=IN_PROJ_TM, tn=IN_PROJ_TN,
                          epilogues=((2 * e, "gelu"), (e, "silu")))
    x2 = _gmlp_out(proj_a, a_v_g, a_ws, a_bs, a_w_out, x2, tm=GMLP_TM)
    proj_b = _norm_matmul(x2, norm_g[1], b_w_in, tm=IN_PROJ_TM, tn=IN_PROJ_TN,
                          epilogues=((e, head_dim ** -0.5 * LOG2E), (2 * e, None), (e, "silu")),
                          col_major=True)
    y = _na_attention(proj_b, rp, bsz=bsz, t=t)
    out = _out_norm(y, b_w_out, x2, final_g, tm=OUT_PROJ_TM)
    return out.reshape(bsz, t, d)


def kernel(x_prompt, x_sample, norm_g, final_g, a_w_in, a_v_g, a_ws, a_bs, a_w_out, b_w_in, b_rpb,
           b_w_out):
    assert norm_g.shape[0] == 2 and a_w_in.shape[0] == 1 and b_w_in.shape[0] == 1
    params = (norm_g, final_g, a_w_in[0].astype(BF16), a_v_g[0], a_ws[0].astype(BF16), a_bs[0],
              a_w_out[0].astype(BF16), b_w_in[0].astype(BF16), _pair_table(b_rpb[0]),
              b_w_out[0].astype(BF16))
    return (_trunk(x_prompt, *params), _trunk(x_sample, *params))
```
